```python
import jax, jax.numpy as jnp
from jax import lax
import numpy as np

D_MODEL = 1024
BATCH = 32
SEQ = 2048
DEPTH = 1

CHUNK = 64
N_MEM = 256

A_HEADS = 8
A_HEAD_DIM = 64
A_WIDTH = A_HEADS * A_HEAD_DIM
DECAY_LORA = 64
AAA_LORA = 64
GATE_LORA = 128
LNX_EPS = 64e-5

B_BLOCKS = 8
B_WIDTH = 512
B_BLOCK_DIM = B_WIDTH // B_BLOCKS
B_CONV = 4
LRU_C = 8.0

X_HEADS = 4
X_HEAD_DIM = D_MODEL // X_HEADS

D_FF = 2816
FFN_CONV = 3

NORM_EPS = 1e-6

RWKV_COLS = 3 * A_WIDTH + DECAY_LORA + AAA_LORA + GATE_LORA
P_IN = RWKV_COLS + 2 * B_WIDTH + 2 * D_MODEL

kernel_name = "hybrid_rwkv7_rglru_gated_block"


def rms_norm(x, g):
    xf = x.astype(jnp.float32)
    y = xf * lax.rsqrt(jnp.mean(xf * xf, axis=-1, keepdims=True) + NORM_EPS)
    return (y * g.astype(jnp.float32)).astype(x.dtype)


def causal_dwconv(x, w, b):
    k_width = w.shape[0]
    s = x.shape[1]
    xp = jnp.pad(x, ((0, 0), (k_width - 1, 0), (0, 0)))
    y = b
    for j in range(k_width):
        y = y + xp[:, j:j + s] * w[j]
    return y


def token_shift(p):
    return jnp.pad(p, ((0, 0), (1, 0), (0, 0)))[:, :-1]


def rwkv7_recurrence(r, decay, k, v, a, b):
    def step(state, inp):
        r_t, w_t, k_t, v_t, a_t, b_t = inp
        sa = jnp.einsum('bhij,bhj->bhi', state, a_t)
        state = (state * w_t[:, :, None, :]
                 + sa[..., None] * b_t[:, :, None, :]
                 + v_t[..., None] * k_t[:, :, None, :])
        y_t = jnp.einsum('bhij,bhj->bhi', state, r_t)
        return state, y_t
    bsz, _, nh, nd = r.shape
    xs = tuple(jnp.swapaxes(t, 0, 1) for t in (r, decay, k, v, a, b))
    s0 = jnp.zeros((bsz, nh, nd, nd), jnp.float32)
    _, ys = lax.scan(step, s0, xs)
    return jnp.swapaxes(ys, 0, 1)


def rwkv7_branch(p, mu, w0, w_up, a0, a_up, g_up, k_k, k_a, r_k, lnx_g, lnx_b):
    bsz, s, _ = p.shape
    p = p.astype(jnp.float32)
    p = p + (token_shift(p) - p) * mu
    c0, c1, c2 = A_WIDTH, 2 * A_WIDTH, 3 * A_WIDTH
    c3, c4 = c2 + DECAY_LORA, c2 + DECAY_LORA + AAA_LORA
    r, k, v = p[..., :c0], p[..., c0:c1], p[..., c1:c2]
    pw, pa, pg = p[..., c2:c3], p[..., c3:c4], p[..., c4:]
    w_log = -jax.nn.softplus(-(w0 + jnp.tanh(pw) @ w_up)) - 0.5
    decay = jnp.exp(-jnp.exp(w_log))
    a = jax.nn.sigmoid(a0 + pa @ a_up)
    g = jax.nn.sigmoid(pg) @ g_up
    kk = k * k_k
    k = k * (1.0 + (a - 1.0) * k_a)
    hs = lambda t: t.reshape(bsz, s, A_HEADS, A_HEAD_DIM)
    r, k, v, kk, a, decay = hs(r), hs(k), hs(v), hs(kk), hs(a), hs(decay)
    kk = kk / jnp.maximum(jnp.sqrt(jnp.sum(kk * kk, axis=-1, keepdims=True)), 1e-12)
    y = rwkv7_recurrence(r, decay, k, v, -kk, kk * a)
    mean = jnp.mean(y, axis=-1, keepdims=True)
    var = jnp.mean(jnp.square(y - mean), axis=-1, keepdims=True)
    y = ((y - mean) * lax.rsqrt(var + LNX_EPS)).reshape(bsz, s, A_WIDTH) * lnx_g + lnx_b
    bonus = jnp.sum(r * k * r_k, axis=-1, keepdims=True) * v
    return (y + bonus.reshape(bsz, s, A_WIDTH)) * g


def _lin_combine(c1, c2):
    a1, b1 = c1
    a2, b2 = c2
    return a1 * a2, a2 * b1 + b2


def rglru_branch(px, py, conv_w, conv_b, w_ga, b_ga, w_gx, b_gx, lam):
    bsz, s, _ = px.shape
    xb = causal_dwconv(px.astype(jnp.float32), conv_w, conv_b)
    xh = xb.reshape(bsz, s, B_BLOCKS, B_BLOCK_DIM)
    gate_a = jax.nn.sigmoid(jnp.einsum('bshi,hij->bshj', xh, w_ga).reshape(bsz, s, B_WIDTH) + b_ga)
    gate_x = jax.nn.sigmoid(jnp.einsum('bshi,hij->bshj', xh, w_gx).reshape(bsz, s, B_WIDTH) + b_gx)
    log_a = LRU_C * gate_a * jax.nn.log_sigmoid(lam)
    a = jnp.exp(log_a)
    mult = jnp.sqrt(-jnp.expm1(2.0 * log_a))
    pos = jnp.arange(s)[None, :, None]
    mult = jnp.where(pos == 0, 1.0, mult)
    u = xb * gate_x * mult
    _, h = lax.associative_scan(_lin_combine, (a, u), axis=1)
    return h * jax.nn.gelu(py.astype(jnp.float32))


def cross_attention(h, m, w_q, w_kv, w_o):
    bsz, s, _ = h.shape
    q = (h @ w_q).reshape(bsz, s, X_HEADS, X_HEAD_DIM)
    kv = m @ w_kv
    k = kv[..., :D_MODEL].reshape(bsz, -1, X_HEADS, X_HEAD_DIM)
    v = kv[..., D_MODEL:].reshape(bsz, -1, X_HEADS, X_HEAD_DIM)
    sc = jnp.einsum('bqhd,bkhd->bhqk', q, k).astype(jnp.float32) * (X_HEAD_DIM ** -0.5)
    pr = jax.nn.softmax(sc, axis=-1).astype(v.dtype)
    o = jnp.einsum('bhqk,bkhd->bqhd', pr, v).reshape(bsz, s, D_MODEL)
    return o @ w_o


def conv_ffn(h, w_in, conv_w, conv_b, w_out):
    u = h @ w_in
    gate = causal_dwconv(u[..., :D_FF], conv_w, conv_b)
    return (jax.nn.gelu(gate) * u[..., D_FF:]) @ w_out


def setup_inputs(seed: int = 0) -> dict:
    key = jax.random.key(seed)
    ks = iter(jax.random.split(key, 48))
    L = DEPTH
    f32 = jnp.float32

    def nrm(shape, scale):
        return jax.random.normal(next(ks), shape, f32) * scale

    u_lam = jax.random.uniform(next(ks), (L, B_WIDTH), f32, minval=0.9, maxval=0.999)
    return {
        "x": nrm((BATCH, SEQ, D_MODEL), 1.0),
        "mem": nrm((BATCH, N_MEM, D_MODEL), 1.0),
        "norm_mix_g": 1.0 + nrm((L, D_MODEL), 0.05),
        "w_in": nrm((L, D_MODEL, P_IN), D_MODEL ** -0.5),
        "b_in": nrm((L, P_IN), 0.02),
        "mu_shift": jax.random.uniform(next(ks), (L, RWKV_COLS), f32),
        "w0": nrm((L, A_WIDTH), 0.5) - 0.5,
        "w_lora_up": nrm((L, DECAY_LORA, A_WIDTH), 0.1),
        "a0": nrm((L, A_WIDTH), 0.5),
        "a_lora_up": nrm((L, AAA_LORA, A_WIDTH), AAA_LORA ** -0.5),
        "g_lora_up": nrm((L, GATE_LORA, A_WIDTH), GATE_LORA ** -0.5),
        "k_k": 0.85 + nrm((L, A_WIDTH), 0.05),
        "k_a": 1.0 + nrm((L, A_WIDTH), 0.05),
        "r_k": nrm((L, A_HEADS, A_HEAD_DIM), 0.1),
        "lnx_g": 1.0 + nrm((L, A_WIDTH), 0.05),
        "lnx_b": nrm((L, A_WIDTH), 0.02),
        "w_branch_a": nrm((L, A_WIDTH, D_MODEL), A_WIDTH ** -0.5),
        "conv_b_w": nrm((L, B_CONV, B_WIDTH), B_CONV ** -0.5),
        "conv_b_b": nrm((L, B_WIDTH), 0.02),
        "w_rg_a": nrm((L, B_BLOCKS, B_BLOCK_DIM, B_BLOCK_DIM), B_BLOCK_DIM ** -0.5),
        "b_rg_a": nrm((L, B_WIDTH), 0.02),
        "w_rg_x": nrm((L, B_BLOCKS, B_BLOCK_DIM, B_BLOCK_DIM), B_BLOCK_DIM ** -0.5),
        "b_rg_x": nrm((L, B_WIDTH), 0.02),
        "lru_lambda": jnp.log(u_lam) - jnp.log1p(-u_lam),
        "w_branch_b": nrm((L, B_WIDTH, D_MODEL), B_WIDTH ** -0.5),
        "w_mix_out": nrm((L, D_MODEL, D_MODEL), D_MODEL ** -0.5),
        "norm_x_g": 1.0 + nrm((L, D_MODEL), 0.05),
        "norm_mem_g": 1.0 + nrm((L, D_MODEL), 0.05),
        "w_cq": nrm((L, D_MODEL, D_MODEL), D_MODEL ** -0.5),
        "w_ckv": nrm((L, D_MODEL, 2 * D_MODEL), D_MODEL ** -0.5),
        "w_co": nrm((L, D_MODEL, D_MODEL), D_MODEL ** -0.5),
        "norm_ffn_g": 1.0 + nrm((L, D_MODEL), 0.05),
        "w_ffn_in": nrm((L, D_MODEL, 2 * D_FF), D_MODEL ** -0.5),
        "ffn_conv_w": nrm((L, FFN_CONV, D_FF), FFN_CONV ** -0.5),
        "ffn_conv_b": nrm((L, D_FF), 0.02),
        "w_ffn_out": nrm((L, D_FF, D_MODEL), D_FF ** -0.5),
        "norm_final_g": 1.0 + nrm((D_MODEL,), 0.05),
    }


def reference(x, mem, norm_mix_g, w_in, b_in, mu_shift, w0, w_lora_up, a0, a_lora_up,
              g_lora_up, k_k, k_a, r_k, lnx_g, lnx_b, w_branch_a, conv_b_w, conv_b_b,
              w_rg_a, b_rg_a, w_rg_x, b_rg_x, lru_lambda, w_branch_b, w_mix_out,
              norm_x_g, norm_mem_g, w_cq, w_ckv, w_co, norm_ffn_g, w_ffn_in,
              ffn_conv_w, ffn_conv_b, w_ffn_out, norm_final_g):
    dt = x.dtype
    cb0 = RWKV_COLS
    cb1 = cb0 + B_WIDTH
    cb2 = cb1 + B_WIDTH
    cg1 = cb2 + D_MODEL
    for l in range(DEPTH):
        h = rms_norm(x, norm_mix_g[l])
        proj = h @ w_in[l] + b_in[l]
        y_a = rwkv7_branch(proj[..., :cb0], mu_shift[l], w0[l], w_lora_up[l], a0[l],
                           a_lora_up[l], g_lora_up[l], k_k[l], k_a[l], r_k[l],
                           lnx_g[l], lnx_b[l]).astype(dt) @ w_branch_a[l]
        y_b = rglru_branch(proj[..., cb0:cb1], proj[..., cb1:cb2], conv_b_w[l], conv_b_b[l],
                           w_rg_a[l], b_rg_a[l], w_rg_x[l], b_rg_x[l],
                           lru_lambda[l]).astype(dt) @ w_branch_b[l]
        merged = (jax.nn.sigmoid(proj[..., cb2:cg1]) * y_a
                  + jax.nn.sigmoid(proj[..., cg1:]) * y_b)
        x = x + merged @ w_mix_out[l]
        x = x + cross_attention(rms_norm(x, norm_x_g[l]), rms_norm(mem, norm_mem_g[l]),
                                w_cq[l], w_ckv[l], w_co[l])
        x = x + conv_ffn(rms_norm(x, norm_ffn_g[l]), w_ffn_in[l], ffn_conv_w[l],
                         ffn_conv_b[l], w_ffn_out[l])
    return rms_norm(x, norm_final_g)
```

```python
import functools
import math

import jax
import jax.numpy as jnp
from jax import lax
from jax.experimental import pallas as pl
from jax.experimental.pallas import tpu as pltpu

F32 = jnp.float32
BF16 = jnp.bfloat16

NORM_EPS = 1e-6
LNX_EPS = 64e-5
LRU_C = 8.0
HEAD = 64
CHUNK = 64
SUBLANES = 8
VMEM_LIMIT = 56 * 1024 * 1024


def _rms(x, g):
    return x * lax.rsqrt(jnp.mean(x * x, axis=-1, keepdims=True) + NORM_EPS) * g


def _sigmoid(x):
    return 1.0 / (1.0 + jnp.exp(-x))


def _gelu(x):
    c = math.sqrt(2.0 / math.pi)
    return 0.5 * x * (1.0 + jnp.tanh(c * (x + 0.044715 * (x * x * x))))


def _bdot(a, b):
    return jnp.dot(a.astype(BF16), b.astype(BF16), preferred_element_type=F32)


def _bdot_nt(a, b):
    return lax.dot_general(a.astype(BF16), b.astype(BF16), (((1,), (1,)), ((), ())),
                           preferred_element_type=F32)


def _bdot_tn(a, b):
    return lax.dot_general(a.astype(BF16), b.astype(BF16), (((0,), (0,)), ((), ())),
                           preferred_element_type=F32)


def _split_dot(x, m):
    hi = x.astype(BF16)
    lo = (x - hi.astype(F32)).astype(BF16)
    return (jnp.dot(hi, m, preferred_element_type=F32)
            + jnp.dot(lo, m, preferred_element_type=F32))


def _split3_dot_left(m, x):
    p0 = x.astype(BF16)
    r1 = x - p0.astype(F32)
    p1 = r1.astype(BF16)
    p2 = (r1 - p1.astype(F32)).astype(BF16)
    return (jnp.dot(m, p0, preferred_element_type=F32)
            + jnp.dot(m, p1, preferred_element_type=F32)
            + jnp.dot(m, p2, preferred_element_type=F32))


def _const_spec(shape):
    n = len(shape)
    return pl.BlockSpec(shape, lambda *_: (0,) * n)


def _rwkv_body(x_ref, g_ref, w_ref, b_ref, mu_ref, w0_ref, wup_ref, a0_ref, aup_ref, gup_ref,
               kk_ref, ka_ref, rk_ref, lg_ref, lb_ref, bd_ref, tri_ref, o_ref,
               carry_ref, p_ref, at_ref, rt_ref, bt_ref, kt_ref, v_ref, cl_ref, y_ref,
               *, ts, width):
    s = pl.program_id(1)
    nheads = width // HEAD
    c1, c2, c3 = width, 2 * width, 3 * width

    @pl.when(s == 0)
    def _():
        carry_ref[...] = jnp.zeros_like(carry_ref)
        p_ref[...] = jnp.zeros_like(p_ref)

    h = _rms(x_ref[...], g_ref[...]).astype(BF16)
    p = jnp.dot(h, w_ref[...], preferred_element_type=F32) + b_ref[...]
    row = lax.broadcasted_iota(jnp.int32, (ts, 1), 0)
    prev = jnp.where(row == 0, carry_ref[...], pltpu.roll(p, 1, 0))
    carry_ref[...] = p[ts - 1:ts, :]
    p = p + (prev - p) * mu_ref[...]

    r = p[:, :c1]
    k = p[:, c1:c2]
    v = p[:, c2:c3]
    lora = p[:, c3:c3 + 128]
    pg = p[:, c3 + 128:]

    w_pre = w0_ref[...] + _bdot(jnp.tanh(lora), wup_ref[...])
    z = -w_pre
    w_log = -(jnp.maximum(z, 0.0) + jnp.log(1.0 + jnp.exp(-jnp.abs(z)))) - 0.5
    ew = jnp.exp(w_log)
    a = _sigmoid(a0_ref[...] + _bdot(lora, aup_ref[...]))
    g = _bdot(_sigmoid(pg), gup_ref[...])
    kk = k * kk_ref[...]
    k2 = k * (1.0 + (a - 1.0) * ka_ref[...])
    bd = bd_ref[...]
    ss = _split_dot(kk * kk, bd)
    kk = kk / jnp.maximum(jnp.sqrt(ss), 1e-12)
    bonus = _split_dot(r * k2 * rk_ref[...], bd) * v

    cl = _split3_dot_left(tri_ref[...], ew)
    w_incl = jnp.exp(-cl)
    w_inv = jnp.exp(cl)
    at_ref[...] = -kk * jnp.exp(ew - cl)
    rt_ref[...] = r * w_incl
    bt_ref[...] = (kk * a) * w_inv
    kt_ref[...] = k2 * w_inv
    v_ref[...] = v
    cl_ref[...] = cl

    C = CHUNK
    ti = lax.broadcasted_iota(jnp.int32, (C, C), 0)
    si = lax.broadcasted_iota(jnp.int32, (C, C), 1)
    strict = ti > si
    incl = ti >= si
    eye = (ti == si).astype(F32)
    level_masks = []
    m = 2
    while m < C:
        level_masks.append(((ti // m) % 2 == 1) & ((si // m) % 2 == 0) & (ti // (2 * m) == si // (2 * m)))
        m *= 2
    base_mask = (ti // 2 == si // 2) & strict

    def chunk_step(c, carry):
        r0 = pl.multiple_of(c * C, C)
        at = at_ref[pl.ds(r0, C), :]
        rt = rt_ref[pl.ds(r0, C), :]
        bt = bt_ref[pl.ds(r0, C), :]
        kt = kt_ref[pl.ds(r0, C), :]
        vv = v_ref[pl.ds(r0, C), :]
        wc = jnp.exp(-cl_ref[pl.ds(r0 + C - 1, 1), :])
        for hd in range(nheads):
            sl = slice(hd * HEAD, (hd + 1) * HEAD)
            a_h, r_h, b_h, k_h, v_h = at[:, sl], rt[:, sl], bt[:, sl], kt[:, sl], vv[:, sl]
            wc_h = wc[:, sl]
            P = p_ref[:, sl]
            lhs = jnp.concatenate([a_h, r_h], axis=0)
            rhs = jnp.concatenate([b_h, k_h], axis=0)
            A = _bdot_nt(lhs, rhs)
            n_ab = jnp.where(strict, A[:C, :C], 0.0)
            a_ak = jnp.where(strict, A[:C, C:], 0.0)
            a_rb = jnp.where(incl, A[C:, :C], 0.0)
            a_rk = jnp.where(incl, A[C:, C:], 0.0)
            T = eye + jnp.where(base_mask, n_ab, 0.0)
            for lm in level_masks:
                T = T + _bdot(_bdot(T, jnp.where(lm, n_ab, 0.0)), T)
            U = _bdot(T, _bdot(a_h, P) + _bdot(a_ak, v_h))
            y = _bdot(r_h, P) + _bdot(a_rb, U) + _bdot(a_rk, v_h)
            y_ref[pl.ds(r0, C), sl] = y
            wc_col = jnp.sum(eye * wc_h, axis=1, keepdims=True)
            p_ref[:, sl] = P * wc_col + _bdot_tn(b_h * wc_h, U) + _bdot_tn(k_h * wc_h, v_h)
        return carry

    lax.fori_loop(0, ts // C, chunk_step, 0)

    y = y_ref[...]
    inv_n = 1.0 / HEAD
    mean = _split_dot(y, bd) * inv_n
    yc = y - mean
    var = _split_dot(yc * yc, bd) * inv_n
    yn = yc * lax.rsqrt(var + LNX_EPS) * lg_ref[...] + lb_ref[...]
    o_ref[...] = ((yn + bonus) * g).astype(o_ref.dtype)


def _rwkv_call(x, g, w, b, mu, w0, wup, a0, aup, gup, k_k, k_a, r_k, lnx_g, lnx_b, *, ts):
    B, S, D = x.shape
    width = w0.shape[-1]
    cols = w.shape[-1]
    bd = jnp.kron(jnp.eye(width // HEAD, dtype=F32), jnp.ones((HEAD, HEAD), F32)).astype(BF16)
    tri = jnp.kron(jnp.eye(ts // CHUNK, dtype=F32), jnp.tril(jnp.ones((CHUNK, CHUNK), F32))).astype(BF16)
    row = lambda t: t.reshape(1, -1).astype(F32)
    consts = [row(g), w.astype(BF16), row(b), row(mu), row(w0), wup, row(a0), aup, gup,
              row(k_k), row(k_a), row(r_k), row(lnx_g), row(lnx_b), bd, tri]
    body = functools.partial(_rwkv_body, ts=ts, width=width)
    sc = lambda: pltpu.VMEM((ts, width), F32)
    return pl.pallas_call(
        body,
        grid=(B, S // ts),
        in_specs=[pl.BlockSpec((None, ts, D), lambda i, j: (i, j, 0))] + [_const_spec(c.shape) for c in consts],
        out_specs=pl.BlockSpec((None, ts, width), lambda i, j: (i, j, 0)),
        out_shape=jax.ShapeDtypeStruct((B, S, width), BF16),
        scratch_shapes=[pltpu.VMEM((1, cols), F32), pltpu.VMEM((HEAD, width), F32),
                        sc(), sc(), sc(), sc(), sc(), sc(), sc()],
        compiler_params=pltpu.CompilerParams(dimension_semantics=("arbitrary", "arbitrary"),
                                             vmem_limit_bytes=VMEM_LIMIT),
        name="rwkv",
    )(x, *consts)


def _rglru_body(x_ref, g_ref, w_ref, b_ref, cw_ref, cb_ref, wg_ref, bg_ref, lam_ref, o_ref,
                xs_ref, hc_ref, *, ts, width, kconv):
    s = pl.program_id(1)

    @pl.when(s == 0)
    def _():
        xs_ref[0:SUBLANES, :] = jnp.zeros((SUBLANES, width), F32)
        hc_ref[...] = jnp.zeros_like(hc_ref)

    h = _rms(x_ref[...], g_ref[...]).astype(BF16)
    pxy = jnp.dot(h, w_ref[...], preferred_element_type=F32) + b_ref[...]
    px = pxy[:, :width]
    py = pxy[:, width:]
    xs_ref[SUBLANES:SUBLANES + ts, :] = px
    xb = cb_ref[...]
    for j in range(kconv):
        xb = xb + xs_ref[pl.ds(SUBLANES - (kconv - 1) + j, ts), :] * cw_ref[j:j + 1, :]
    xs_ref[0:SUBLANES, :] = px[ts - SUBLANES:ts, :]

    gates = _bdot(xb, wg_ref[...]) + bg_ref[...]
    ga = _sigmoid(gates[:, :width])
    gx = _sigmoid(gates[:, width:])
    lam = lam_ref[...]
    log_sig = jnp.minimum(lam, 0.0) - jnp.log(1.0 + jnp.exp(-jnp.abs(lam)))
    log_a = LRU_C * ga * log_sig
    a = jnp.exp(log_a)
    mult = jnp.sqrt(1.0 - jnp.exp(2.0 * log_a))
    row = lax.broadcasted_iota(jnp.int32, (ts, 1), 0)
    mult = jnp.where((row == 0) & (s == 0), 1.0, mult)
    u = xb * gx * mult
    u = u + jnp.where(row == 0, a * hc_ref[...], 0.0)
    d = 1
    while d < ts:
        keep = row >= d
        u_sh = pltpu.roll(u, d, 0)
        a_sh = pltpu.roll(a, d, 0)
        u = u + jnp.where(keep, a * u_sh, 0.0)
        a = jnp.where(keep, a * a_sh, a)
        d *= 2
    hc_ref[...] = u[ts - 1:ts, :]
    o_ref[...] = (u * _gelu(py)).astype(o_ref.dtype)


def _rglru_call(x, g, w, b, conv_w, conv_b, w_ga, b_ga, w_gx, b_gx, lam, *, ts):
    B, S, D = x.shape
    width = lam.shape[-1]
    nb = w_ga.shape[0]
    bdiag = lambda t: jax.scipy.linalg.block_diag(*[t[i] for i in range(nb)])
    wg = jnp.concatenate([bdiag(w_ga), bdiag(w_gx)], axis=1).astype(BF16)
    row = lambda t: t.reshape(1, -1).astype(F32)
    consts = [row(g), w.astype(BF16), row(b), conv_w.astype(F32), row(conv_b), wg,
              jnp.concatenate([row(b_ga), row(b_gx)], axis=1), row(lam)]
    body = functools.partial(_rglru_body, ts=ts, width=width, kconv=conv_w.shape[0])
    return pl.pallas_call(
        body,
        grid=(B, S // ts),
        in_specs=[pl.BlockSpec((None, ts, D), lambda i, j: (i, j, 0))] + [_const_spec(c.shape) for c in consts],
        out_specs=pl.BlockSpec((None, ts, width), lambda i, j: (i, j, 0)),
        out_shape=jax.ShapeDtypeStruct((B, S, width), BF16),
        scratch_shapes=[pltpu.VMEM((ts + SUBLANES, width), F32), pltpu.VMEM((1, width), F32)],
        compiler_params=pltpu.CompilerParams(dimension_semantics=("arbitrary", "arbitrary"),
                                             vmem_limit_bytes=VMEM_LIMIT),
        name="rglru",
    )(x, *consts)


def _merge_body(x_ref, ya_ref, yb_ref, g_ref, wg_ref, bg_ref, wa_ref, wb_ref, wm_ref, o_ref, *, d):
    x = x_ref[...]
    h = _rms(x, g_ref[...]).astype(BF16)
    gates = jnp.dot(h, wg_ref[...], preferred_element_type=F32) + bg_ref[...]
    ya = jnp.dot(ya_ref[...], wa_ref[...], preferred_element_type=F32)
    yb = jnp.dot(yb_ref[...], wb_ref[...], preferred_element_type=F32)
    merged = _sigmoid(gates[:, :d]) * ya + _sigmoid(gates[:, d:]) * yb
    o_ref[...] = x + jnp.dot(merged.astype(BF16), wm_ref[...], preferred_element_type=F32)


def _merge_call(x, ya, yb, g, wg, bg, wa, wb, wm, *, tm):
    B, S, D = x.shape
    width = ya.shape[-1]
    consts = [g.reshape(1, -1).astype(F32), wg.astype(BF16), bg.reshape(1, -1).astype(F32),
              wa.astype(BF16), wb.astype(BF16), wm.astype(BF16)]
    tok = lambda n: pl.BlockSpec((None, tm, n), lambda i, j: (i, j, 0))
    return pl.pallas_call(
        functools.partial(_merge_body, d=D),
        grid=(B, S // tm),
        in_specs=[tok(D), tok(width), tok(width)] + [_const_spec(c.shape) for c in consts],
        out_specs=tok(D),
        out_shape=jax.ShapeDtypeStruct((B, S, D), F32),
        compiler_params=pltpu.CompilerParams(dimension_semantics=("arbitrary", "arbitrary"),
                                             vmem_limit_bytes=VMEM_LIMIT),
        name="merge",
    )(x, ya, yb, *consts)


def _kv_body(m_ref, g_ref, w_ref, o_ref):
    h = _rms(m_ref[...], g_ref[...]).astype(BF16)
    o_ref[...] = jnp.dot(h, w_ref[...], preferred_element_type=F32).astype(o_ref.dtype)


def _kv_call(mem, g, w):
    B, M, D = mem.shape
    consts = [g.reshape(1, -1).astype(F32), w.astype(BF16)]
    return pl.pallas_call(
        _kv_body,
        grid=(B,),
        in_specs=[pl.BlockSpec((None, M, D), lambda i: (i, 0, 0))] + [_const_spec(c.shape) for c in consts],
        out_specs=pl.BlockSpec((None, M, 2 * D), lambda i: (i, 0, 0)),
        out_shape=jax.ShapeDtypeStruct((B, M, 2 * D), BF16),
        compiler_params=pltpu.CompilerParams(dimension_semantics=("arbitrary",),
                                             vmem_limit_bytes=VMEM_LIMIT),
        name="memkv",
    )(mem, *consts)


def _attn_body(x_ref, kv_ref, g_ref, wq_ref, wo_ref, o_ref, *, d, heads):
    x = x_ref[...]
    h = _rms(x, g_ref[...]).astype(BF16)
    q = jnp.dot(h, wq_ref[...], preferred_element_type=F32)
    hd = d // heads
    scale = hd ** -0.5
    outs = []
    for i in range(heads):
        qh = q[:, i * hd:(i + 1) * hd].astype(BF16)
        kh = kv_ref[:, i * hd:(i + 1) * hd]
        vh = kv_ref[:, d + i * hd:d + (i + 1) * hd]
        sc = lax.dot_general(qh, kh, (((1,), (1,)), ((), ())), preferred_element_type=F32) * scale
        sc = sc - jnp.max(sc, axis=-1, keepdims=True)
        e = jnp.exp(sc)
        pr = e / jnp.sum(e, axis=-1, keepdims=True)
        outs.append(jnp.dot(pr.astype(BF16), vh, preferred_element_type=F32).astype(BF16))
    o = jnp.concatenate(outs, axis=1)
    o_ref[...] = x + jnp.dot(o, wo_ref[...], preferred_element_type=F32)


def _attn_call(x, kv, g, wq, wo, *, tm, heads):
    B, S, D = x.shape
    M = kv.shape[1]
    consts = [g.reshape(1, -1).astype(F32), wq.astype(BF16), wo.astype(BF16)]
    tok = pl.BlockSpec((None, tm, D), lambda i, j: (i, j, 0))
    return pl.pallas_call(
        functools.partial(_attn_body, d=D, heads=heads),
        grid=(B, S // tm),
        in_specs=[tok, pl.BlockSpec((None, M, 2 * D), lambda i, j: (i, 0, 0))]
        + [_const_spec(c.shape) for c in consts],
        out_specs=tok,
        out_shape=jax.ShapeDtypeStruct((B, S, D), F32),
        compiler_params=pltpu.CompilerParams(dimension_semantics=("arbitrary", "arbitrary"),
                                             vmem_limit_bytes=VMEM_LIMIT),
        name="attn",
    )(x, kv, *consts)


def _ffn_body(x_ref, g_ref, wi_ref, cw_ref, cb_ref, wo_ref, gf_ref, o_ref, us_ref,
              *, tm, dff, ck, kconv, final_norm):
    s = pl.program_id(1)

    @pl.when(s == 0)
    def _():
        us_ref[0:SUBLANES, :] = jnp.zeros((SUBLANES, dff), F32)

    x = x_ref[...]
    h = _rms(x, g_ref[...]).astype(BF16)
    acc = x
    for c0 in range(0, dff, ck):
        ug = jnp.dot(h, wi_ref[:, c0:c0 + ck], preferred_element_type=F32)
        uv = jnp.dot(h, wi_ref[:, dff + c0:dff + c0 + ck], preferred_element_type=F32)
        us_ref[SUBLANES:SUBLANES + tm, c0:c0 + ck] = ug
        gate = cb_ref[:, c0:c0 + ck]
        for j in range(kconv):
            gate = gate + (us_ref[pl.ds(SUBLANES - (kconv - 1) + j, tm), c0:c0 + ck]
                           * cw_ref[j:j + 1, c0:c0 + ck])
        us_ref[0:SUBLANES, c0:c0 + ck] = ug[tm - SUBLANES:tm, :]
        act = (_gelu(gate) * uv).astype(BF16)
        acc = acc + jnp.dot(act, wo_ref[c0:c0 + ck, :], preferred_element_type=F32)
    if final_norm:
        acc = _rms(acc, gf_ref[...])
    o_ref[...] = acc


def _ffn_call(x, g, wi, conv_w, conv_b, wo, gf, *, tm, ck, final_norm):
    B, S, D = x.shape
    dff = wo.shape[0]
    consts = [g.reshape(1, -1).astype(F32), wi.astype(BF16), conv_w.astype(F32),
              conv_b.reshape(1, -1).astype(F32), wo.astype(BF16), gf.reshape(1, -1).astype(F32)]
    tok = pl.BlockSpec((None, tm, D), lambda i, j: (i, j, 0))
    body = functools.partial(_ffn_body, tm=tm, dff=dff, ck=ck, kconv=conv_w.shape[0], final_norm=final_norm)
    return pl.pallas_call(
        body,
        grid=(B, S // tm),
        in_specs=[tok] + [_const_spec(c.shape) for c in consts],
        out_specs=tok,
        out_shape=jax.ShapeDtypeStruct((B, S, D), F32),
        scratch_shapes=[pltpu.VMEM((tm + SUBLANES, dff), F32)],
        compiler_params=pltpu.CompilerParams(dimension_semantics=("arbitrary", "arbitrary"),
                                             vmem_limit_bytes=VMEM_LIMIT),
        name="ffn",
    )(x, *consts)


def _pick(n, pref):
    t = min(n, pref)
    while n % t:
        t //= 2
    return t


def kernel(x, mem, norm_mix_g, w_in, b_in, mu_shift, w0, w_lora_up, a0, a_lora_up, g_lora_up, k_k, k_a, r_k, lnx_g, lnx_b, w_branch_a, conv_b_w, conv_b_b, w_rg_a, b_rg_a, w_rg_x, b_rg_x, lru_lambda, w_branch_b, w_mix_out, norm_x_g, norm_mem_g, w_cq, w_ckv, w_co, norm_ffn_g, w_ffn_in, ffn_conv_w, ffn_conv_b, w_ffn_out, norm_final_g):
    depth = w_in.shape[0]
    B, S, D = x.shape
    a_width = w0.shape[-1]
    b_width = lru_lambda.shape[-1]
    n_dec, n_aaa = w_lora_up.shape[1], a_lora_up.shape[1]
    assert n_dec + n_aaa == 128, "decay/a lora widths must fill one 128-lane slab"
    cb0 = mu_shift.shape[-1]
    cb2 = cb0 + 2 * b_width
    heads = 4
    ts = _pick(S, 256)
    tm = _pick(S, 512)
    for l in range(depth):
        wup = jnp.concatenate([w_lora_up[l], jnp.zeros((n_aaa, a_width), F32)], axis=0).astype(BF16)
        aup = jnp.concatenate([jnp.zeros((n_dec, a_width), F32), a_lora_up[l]], axis=0).astype(BF16)
        ya = _rwkv_call(x, norm_mix_g[l], w_in[l][:, :cb0], b_in[l][:cb0], mu_shift[l], w0[l], wup, a0[l], aup,
                        g_lora_up[l].astype(BF16), k_k[l], k_a[l], r_k[l], lnx_g[l], lnx_b[l], ts=ts)
        yb = _rglru_call(x, norm_mix_g[l], w_in[l][:, cb0:cb2], b_in[l][cb0:cb2], conv_b_w[l], conv_b_b[l],
                         w_rg_a[l], b_rg_a[l], w_rg_x[l], b_rg_x[l], lru_lambda[l], ts=ts)
        x = _merge_call(x, ya, yb, norm_mix_g[l], w_in[l][:, cb2:], b_in[l][cb2:], w_branch_a[l], w_branch_b[l],
                        w_mix_out[l], tm=tm)
        kv = _kv_call(mem, norm_mem_g[l], w_ckv[l])
        x = _attn_call(x, kv, norm_x_g[l], w_cq[l], w_co[l], tm=tm, heads=heads)
        last = l == depth - 1
        x = _ffn_call(x, norm_ffn_g[l], w_ffn_in[l], ffn_conv_w[l], ffn_conv_b[l], w_ffn_out[l], norm_final_g,
                      tm=_pick(S, 256), ck=256, final_norm=last)
    return x
```

```python
import functools
import math

import jax
import jax.numpy as jnp
from jax import lax
from jax.experimental import pallas as pl
from jax.experimental.pallas import tpu as pltpu

F32 = jnp.float32
BF16 = jnp.bfloat16

NORM_EPS = 1e-6
LNX_EPS = 64e-5
LRU_C = 8.0
HEAD = 64
CHUNK = 64
SUBLANES = 8
VMEM_LIMIT = 56 * 1024 * 1024


def _rms(x, g):
    return x * lax.rsqrt(jnp.mean(x * x, axis=-1, keepdims=True) + NORM_EPS) * g


def _sigmoid(x):
    return 1.0 / (1.0 + jnp.exp(-x))


def _gelu(x):
    c = math.sqrt(2.0 / math.pi)
    return 0.5 * x * (1.0 + jnp.tanh(c * (x + 0.044715 * (x * x * x))))


def _bdot(a, b):
    return jnp.dot(a.astype(BF16), b.astype(BF16), preferred_element_type=F32)


def _bdot_nt(a, b):
    return lax.dot_general(a.astype(BF16), b.astype(BF16), (((1,), (1,)), ((), ())),
                           preferred_element_type=F32)


def _bdot_tn(a, b):
    return lax.dot_general(a.astype(BF16), b.astype(BF16), (((0,), (0,)), ((), ())),
                           preferred_element_type=F32)


def _split_dot(x, m):
    hi = x.astype(BF16)
    lo = (x - hi.astype(F32)).astype(BF16)
    return (jnp.dot(hi, m, preferred_element_type=F32)
            + jnp.dot(lo, m, preferred_element_type=F32))


def _split3_dot_left(m, x):
    p0 = x.astype(BF16)
    r1 = x - p0.astype(F32)
    p1 = r1.astype(BF16)
    p2 = (r1 - p1.astype(F32)).astype(BF16)
    return (jnp.dot(m, p0, preferred_element_type=F32)
            + jnp.dot(m, p1, preferred_element_type=F32)
            + jnp.dot(m, p2, preferred_element_type=F32))


def _const_spec(shape):
    n = len(shape)
    return pl.BlockSpec(shape, lambda *_: (0,) * n)


def _rwkv_body(x_ref, g_ref, w_ref, b_ref, mu_ref, w0_ref, wup_ref, a0_ref, aup_ref, gup_ref,
               kk_ref, ka_ref, rk_ref, lg_ref, lb_ref, bd_ref, tri_ref, o_ref,
               carry_ref, p_ref, at_ref, rt_ref, bt_ref, kt_ref, v_ref, cl_ref, y_ref,
               *, ts, width):
    s = pl.program_id(1)
    nheads = width // HEAD
    c1, c2, c3 = width, 2 * width, 3 * width

    @pl.when(s == 0)
    def _():
        carry_ref[...] = jnp.zeros_like(carry_ref)
        p_ref[...] = jnp.zeros_like(p_ref)

    h = _rms(x_ref[...], g_ref[...]).astype(BF16)
    p = jnp.dot(h, w_ref[...], preferred_element_type=F32) + b_ref[...]
    row = lax.broadcasted_iota(jnp.int32, (ts, 1), 0)
    prev = jnp.where(row == 0, carry_ref[...], pltpu.roll(p, 1, 0))
    carry_ref[...] = p[ts - 1:ts, :]
    p = p + (prev - p) * mu_ref[...]

    r = p[:, :c1]
    k = p[:, c1:c2]
    v = p[:, c2:c3]
    lora = p[:, c3:c3 + 128]
    pg = p[:, c3 + 128:]

    w_pre = w0_ref[...] + _bdot(jnp.tanh(lora), wup_ref[...])
    z = -w_pre
    w_log = -(jnp.maximum(z, 0.0) + jnp.log(1.0 + jnp.exp(-jnp.abs(z)))) - 0.5
    ew = jnp.exp(w_log)
    a = _sigmoid(a0_ref[...] + _bdot(lora, aup_ref[...]))
    g = _bdot(_sigmoid(pg), gup_ref[...])
    kk = k * kk_ref[...]
    k2 = k * (1.0 + (a - 1.0) * ka_ref[...])
    bd = bd_ref[...]
    ss = _split_dot(kk * kk, bd)
    kk = kk / jnp.maximum(jnp.sqrt(ss), 1e-12)
    bonus = _split_dot(r * k2 * rk_ref[...], bd) * v

    cl = _split3_dot_left(tri_ref[...], ew)
    w_incl = jnp.exp(-cl)
    w_inv = jnp.exp(cl)
    at_ref[...] = -kk * jnp.exp(ew - cl)
    rt_ref[...] = r * w_incl
    bt_ref[...] = (kk * a) * w_inv
    kt_ref[...] = k2 * w_inv
    v_ref[...] = v
    cl_ref[...] = cl

    C = CHUNK
    ti = lax.broadcasted_iota(jnp.int32, (C, C), 0)
    si = lax.broadcasted_iota(jnp.int32, (C, C), 1)
    strict = ti > si
    incl = ti >= si
    eye = (ti == si).astype(F32)
    level_masks = []
    m = 2
    while m < C:
        level_masks.append(((ti // m) % 2 == 1) & ((si // m) % 2 == 0) & (ti // (2 * m) == si // (2 * m)))
        m *= 2
    base_mask = (ti // 2 == si // 2) & strict

    def chunk_step(c, carry):
        r0 = pl.multiple_of(c * C, C)
        at = at_ref[pl.ds(r0, C), :]
        rt = rt_ref[pl.ds(r0, C), :]
        bt = bt_ref[pl.ds(r0, C), :]
        kt = kt_ref[pl.ds(r0, C), :]
        vv = v_ref[pl.ds(r0, C), :]
        wc = jnp.exp(-cl_ref[pl.ds(r0 + C - 1, 1), :])
        hs = range(nheads)
        sls = [slice(hd * HEAD, (hd + 1) * HEAD) for hd in hs]
        a_h = [at[:, sl] for sl in sls]
        r_h = [rt[:, sl] for sl in sls]
        b_h = [bt[:, sl] for sl in sls]
        k_h = [kt[:, sl] for sl in sls]
        v_h = [vv[:, sl] for sl in sls]
        wc_h = [wc[:, sl] for sl in sls]
        P = [p_ref[:, sl] for sl in sls]
        A = [_bdot_nt(jnp.concatenate([a_h[i], r_h[i]], axis=0),
                      jnp.concatenate([b_h[i], k_h[i]], axis=0)) for i in hs]
        n_ab = [jnp.where(strict, A[i][:C, :C], 0.0) for i in hs]
        a_ak = [jnp.where(strict, A[i][:C, C:], 0.0) for i in hs]
        a_rb = [jnp.where(incl, A[i][C:, :C], 0.0) for i in hs]
        a_rk = [jnp.where(incl, A[i][C:, C:], 0.0) for i in hs]
        T = [eye + jnp.where(base_mask, n_ab[i], 0.0) for i in hs]
        for lm in level_masks:
            TM = [_bdot(T[i], jnp.where(lm, n_ab[i], 0.0)) for i in hs]
            T = [T[i] + _bdot(TM[i], T[i]) for i in hs]
        G = [_bdot(a_h[i], P[i]) + _bdot(a_ak[i], v_h[i]) for i in hs]
        U = [_bdot(T[i], G[i]) for i in hs]
        y = [_bdot(r_h[i], P[i]) + _bdot(a_rb[i], U[i]) + _bdot(a_rk[i], v_h[i]) for i in hs]
        wc_col = [jnp.sum(eye * wc_h[i], axis=1, keepdims=True) for i in hs]
        Pn = [P[i] * wc_col[i] + _bdot_tn(b_h[i] * wc_h[i], U[i]) + _bdot_tn(k_h[i] * wc_h[i], v_h[i])
              for i in hs]
        for i in hs:
            y_ref[pl.ds(r0, C), sls[i]] = y[i]
            p_ref[:, sls[i]] = Pn[i]
        return carry

    lax.fori_loop(0, ts // C, chunk_step, 0)

    y = y_ref[...]
    inv_n = 1.0 / HEAD
    mean = _split_dot(y, bd) * inv_n
    yc = y - mean
    var = _split_dot(yc * yc, bd) * inv_n
    yn = yc * lax.rsqrt(var + LNX_EPS) * lg_ref[...] + lb_ref[...]
    o_ref[...] = ((yn + bonus) * g).astype(o_ref.dtype)


def _rwkv_call(x, g, w, b, mu, w0, wup, a0, aup, gup, k_k, k_a, r_k, lnx_g, lnx_b, *, ts):
    B, S, D = x.shape
    width = w0.shape[-1]
    cols = w.shape[-1]
    bd = jnp.kron(jnp.eye(width // HEAD, dtype=F32), jnp.ones((HEAD, HEAD), F32)).astype(BF16)
    tri = jnp.kron(jnp.eye(ts // CHUNK, dtype=F32), jnp.tril(jnp.ones((CHUNK, CHUNK), F32))).astype(BF16)
    row = lambda t: t.reshape(1, -1).astype(F32)
    consts = [row(g), w.astype(BF16), row(b), row(mu), row(w0), wup, row(a0), aup, gup,
              row(k_k), row(k_a), row(r_k), row(lnx_g), row(lnx_b), bd, tri]
    body = functools.partial(_rwkv_body, ts=ts, width=width)
    sc = lambda: pltpu.VMEM((ts, width), F32)
    return pl.pallas_call(
        body,
        grid=(B, S // ts),
        in_specs=[pl.BlockSpec((None, ts, D), lambda i, j: (i, j, 0))] + [_const_spec(c.shape) for c in consts],
        out_specs=pl.BlockSpec((None, ts, width), lambda i, j: (i, j, 0)),
        out_shape=jax.ShapeDtypeStruct((B, S, width), BF16),
        scratch_shapes=[pltpu.VMEM((1, cols), F32), pltpu.VMEM((HEAD, width), F32),
                        sc(), sc(), sc(), sc(), sc(), sc(), sc()],
        compiler_params=pltpu.CompilerParams(dimension_semantics=("arbitrary", "arbitrary"),
                                             vmem_limit_bytes=VMEM_LIMIT),
        name="rwkv",
    )(x, *consts)


def _rglru_body(x_ref, g_ref, w_ref, b_ref, cw_ref, cb_ref, wg_ref, bg_ref, lam_ref, o_ref,
                xs_ref, hc_ref, *, ts, width, kconv):
    s = pl.program_id(1)

    @pl.when(s == 0)
    def _():
        xs_ref[0:SUBLANES, :] = jnp.zeros((SUBLANES, width), F32)
        hc_ref[...] = jnp.zeros_like(hc_ref)

    h = _rms(x_ref[...], g_ref[...]).astype(BF16)
    pxy = jnp.dot(h, w_ref[...], preferred_element_type=F32) + b_ref[...]
    px = pxy[:, :width]
    py = pxy[:, width:]
    xs_ref[SUBLANES:SUBLANES + ts, :] = px
    xb = cb_ref[...]
    for j in range(kconv):
        xb = xb + xs_ref[pl.ds(SUBLANES - (kconv - 1) + j, ts), :] * cw_ref[j:j + 1, :]
    xs_ref[0:SUBLANES, :] = px[ts - SUBLANES:ts, :]

    gates = _bdot(xb, wg_ref[...]) + bg_ref[...]
    ga = _sigmoid(gates[:, :width])
    gx = _sigmoid(gates[:, width:])
    lam = lam_ref[...]
    log_sig = jnp.minimum(lam, 0.0) - jnp.log(1.0 + jnp.exp(-jnp.abs(lam)))
    log_a = LRU_C * ga * log_sig
    a = jnp.exp(log_a)
    mult = jnp.sqrt(1.0 - jnp.exp(2.0 * log_a))
    row = lax.broadcasted_iota(jnp.int32, (ts, 1), 0)
    mult = jnp.where((row == 0) & (s == 0), 1.0, mult)
    u = xb * gx * mult
    u = u + jnp.where(row == 0, a * hc_ref[...], 0.0)
    d = 1
    while d < ts:
        keep = row >= d
        u_sh = pltpu.roll(u, d, 0)
        a_sh = pltpu.roll(a, d, 0)
        u = u + jnp.where(keep, a * u_sh, 0.0)
        a = jnp.where(keep, a * a_sh, a)
        d *= 2
    hc_ref[...] = u[ts - 1:ts, :]
    o_ref[...] = (u * _gelu(py)).astype(o_ref.dtype)


def _rglru_call(x, g, w, b, conv_w, conv_b, w_ga, b_ga, w_gx, b_gx, lam, *, ts):
    B, S, D = x.shape
    width = lam.shape[-1]
    nb = w_ga.shape[0]
    bdiag = lambda t: jax.scipy.linalg.block_diag(*[t[i] for i in range(nb)])
    wg = jnp.concatenate([bdiag(w_ga), bdiag(w_gx)], axis=1).astype(BF16)
    row = lambda t: t.reshape(1, -1).astype(F32)
    consts = [row(g), w.astype(BF16), row(b), conv_w.astype(F32), row(conv_b), wg,
              jnp.concatenate([row(b_ga), row(b_gx)], axis=1), row(lam)]
    body = functools.partial(_rglru_body, ts=ts, width=width, kconv=conv_w.shape[0])
    return pl.pallas_call(
        body,
        grid=(B, S // ts),
        in_specs=[pl.BlockSpec((None, ts, D), lambda i, j: (i, j, 0))] + [_const_spec(c.shape) for c in consts],
        out_specs=pl.BlockSpec((None, ts, width), lambda i, j: (i, j, 0)),
        out_shape=jax.ShapeDtypeStruct((B, S, width), BF16),
        scratch_shapes=[pltpu.VMEM((ts + SUBLANES, width), F32), pltpu.VMEM((1, width), F32)],
        compiler_params=pltpu.CompilerParams(dimension_semantics=("arbitrary", "arbitrary"),
                                             vmem_limit_bytes=VMEM_LIMIT),
        name="rglru",
    )(x, *consts)


def _merge_body(x_ref, ya_ref, yb_ref, g_ref, wg_ref, bg_ref, wa_ref, wb_ref, wm_ref, o_ref, *, d):
    x = x_ref[...]
    h = _rms(x, g_ref[...]).astype(BF16)
    gates = jnp.dot(h, wg_ref[...], preferred_element_type=F32) + bg_ref[...]
    ya = jnp.dot(ya_ref[...], wa_ref[...], preferred_element_type=F32)
    yb = jnp.dot(yb_ref[...], wb_ref[...], preferred_element_type=F32)
    merged = _sigmoid(gates[:, :d]) * ya + _sigmoid(gates[:, d:]) * yb
    o_ref[...] = x + jnp.dot(merged.astype(BF16), wm_ref[...], preferred_element_type=F32)


def _merge_call(x, ya, yb, g, wg, bg, wa, wb, wm, *, tm):
    B, S, D = x.shape
    width = ya.shape[-1]
    consts = [g.reshape(1, -1).astype(F32), wg.astype(BF16), bg.reshape(1, -1).astype(F32),
              wa.astype(BF16), wb.astype(BF16), wm.astype(BF16)]
    tok = lambda n: pl.BlockSpec((None, tm, n), lambda i, j: (i, j, 0))
    return pl.pallas_call(
        functools.partial(_merge_body, d=D),
        grid=(B, S // tm),
        in_specs=[tok(D), tok(width), tok(width)] + [_const_spec(c.shape) for c in consts],
        out_specs=tok(D),
        out_shape=jax.ShapeDtypeStruct((B, S, D), F32),
        compiler_params=pltpu.CompilerParams(dimension_semantics=("arbitrary", "arbitrary"),
                                             vmem_limit_bytes=VMEM_LIMIT),
        name="merge",
    )(x, ya, yb, *consts)


def _kv_body(m_ref, g_ref, w_ref, o_ref):
    h = _rms(m_ref[...], g_ref[...]).astype(BF16)
    o_ref[...] = jnp.dot(h, w_ref[...], preferred_element_type=F32).astype(o_ref.dtype)


def _kv_call(mem, g, w):
    B, M, D = mem.shape
    consts = [g.reshape(1, -1).astype(F32), w.astype(BF16)]
    return pl.pallas_call(
        _kv_body,
        grid=(B,),
        in_specs=[pl.BlockSpec((None, M, D), lambda i: (i, 0, 0))] + [_const_spec(c.shape) for c in consts],
        out_specs=pl.BlockSpec((None, M, 2 * D), lambda i: (i, 0, 0)),
        out_shape=jax.ShapeDtypeStruct((B, M, 2 * D), BF16),
        compiler_params=pltpu.CompilerParams(dimension_semantics=("arbitrary",),
                                             vmem_limit_bytes=VMEM_LIMIT),
        name="memkv",
    )(mem, *consts)


def _attn_body(x_ref, kv_ref, g_ref, wq_ref, wo_ref, o_ref, *, d, heads):
    x = x_ref[...]
    h = _rms(x, g_ref[...]).astype(BF16)
    q = jnp.dot(h, wq_ref[...], preferred_element_type=F32)
    hd = d // heads
    scale = hd ** -0.5
    outs = []
    for i in range(heads):
        qh = q[:, i * hd:(i + 1) * hd].astype(BF16)
        kh = kv_ref[:, i * hd:(i + 1) * hd]
        vh = kv_ref[:, d + i * hd:d + (i + 1) * hd]
        sc = lax.dot_general(qh, kh, (((1,), (1,)), ((), ())), preferred_element_type=F32) * scale
        sc = sc - jnp.max(sc, axis=-1, keepdims=True)
        e = jnp.exp(sc)
        pr = e / jnp.sum(e, axis=-1, keepdims=True)
        outs.append(jnp.dot(pr.astype(BF16), vh, preferred_element_type=F32).astype(BF16))
    o = jnp.concatenate(outs, axis=1)
    o_ref[...] = x + jnp.dot(o, wo_ref[...], preferred_element_type=F32)


def _attn_call(x, kv, g, wq, wo, *, tm, heads):
    B, S, D = x.shape
    M = kv.shape[1]
    consts = [g.reshape(1, -1).astype(F32), wq.astype(BF16), wo.astype(BF16)]
    tok = pl.BlockSpec((None, tm, D), lambda i, j: (i, j, 0))
    return pl.pallas_call(
        functools.partial(_attn_body, d=D, heads=heads),
        grid=(B, S // tm),
        in_specs=[tok, pl.BlockSpec((None, M, 2 * D), lambda i, j: (i, 0, 0))]
        + [_const_spec(c.shape) for c in consts],
        out_specs=tok,
        out_shape=jax.ShapeDtypeStruct((B, S, D), F32),
        compiler_params=pltpu.CompilerParams(dimension_semantics=("arbitrary", "arbitrary"),
                                             vmem_limit_bytes=VMEM_LIMIT),
        name="attn",
    )(x, kv, *consts)


def _ffn_body(x_ref, g_ref, wi_ref, cw_ref, cb_ref, wo_ref, gf_ref, o_ref, us_ref,
              *, tm, dff, ck, kconv, final_norm):
    s = pl.program_id(1)

    @pl.when(s == 0)
    def _():
        us_ref[0:SUBLANES, :] = jnp.zeros((SUBLANES, dff), F32)

    x = x_ref[...]
    h = _rms(x, g_ref[...]).astype(BF16)
    acc = x
    for c0 in range(0, dff, ck):
        ug = jnp.dot(h, wi_ref[:, c0:c0 + ck], preferred_element_type=F32)
        uv = jnp.dot(h, wi_ref[:, dff + c0:dff + c0 + ck], preferred_element_type=F32)
        us_ref[SUBLANES:SUBLANES + tm, c0:c0 + ck] = ug
        gate = cb_ref[:, c0:c0 + ck]
        for j in range(kconv):
            gate = gate + (us_ref[pl.ds(SUBLANES - (kconv - 1) + j, tm), c0:c0 + ck]
                           * cw_ref[j:j + 1, c0:c0 + ck])
        us_ref[0:SUBLANES, c0:c0 + ck] = ug[tm - SUBLANES:tm, :]
        act = (_gelu(gate) * uv).astype(BF16)
        acc = acc + jnp.dot(act, wo_ref[c0:c0 + ck, :], preferred_element_type=F32)
    if final_norm:
        acc = _rms(acc, gf_ref[...])
    o_ref[...] = acc


def _ffn_call(x, g, wi, conv_w, conv_b, wo, gf, *, tm, ck, final_norm):
    B, S, D = x.shape
    dff = wo.shape[0]
    consts = [g.reshape(1, -1).astype(F32), wi.astype(BF16), conv_w.astype(F32),
              conv_b.reshape(1, -1).astype(F32), wo.astype(BF16), gf.reshape(1, -1).astype(F32)]
    tok = pl.BlockSpec((None, tm, D), lambda i, j: (i, j, 0))
    body = functools.partial(_ffn_body, tm=tm, dff=dff, ck=ck, kconv=conv_w.shape[0], final_norm=final_norm)
    return pl.pallas_call(
        body,
        grid=(B, S // tm),
        in_specs=[tok] + [_const_spec(c.shape) for c in consts],
        out_specs=tok,
        out_shape=jax.ShapeDtypeStruct((B, S, D), F32),
        scratch_shapes=[pltpu.VMEM((tm + SUBLANES, dff), F32)],
        compiler_params=pltpu.CompilerParams(dimension_semantics=("arbitrary", "arbitrary"),
                                             vmem_limit_bytes=VMEM_LIMIT),
        name="ffn",
    )(x, *consts)


def _pick(n, pref):
    t = min(n, pref)
    while n % t:
        t //= 2
    return t


def kernel(x, mem, norm_mix_g, w_in, b_in, mu_shift, w0, w_lora_up, a0, a_lora_up, g_lora_up, k_k, k_a, r_k, lnx_g, lnx_b, w_branch_a, conv_b_w, conv_b_b, w_rg_a, b_rg_a, w_rg_x, b_rg_x, lru_lambda, w_branch_b, w_mix_out, norm_x_g, norm_mem_g, w_cq, w_ckv, w_co, norm_ffn_g, w_ffn_in, ffn_conv_w, ffn_conv_b, w_ffn_out, norm_final_g):
    depth = w_in.shape[0]
    B, S, D = x.shape
    a_width = w0.shape[-1]
    b_width = lru_lambda.shape[-1]
    n_dec, n_aaa = w_lora_up.shape[1], a_lora_up.shape[1]
    assert n_dec + n_aaa == 128, "decay/a lora widths must fill one 128-lane slab"
    cb0 = mu_shift.shape[-1]
    cb2 = cb0 + 2 * b_width
    heads = 4
    ts = _pick(S, 256)
    tm = _pick(S, 512)
    for l in range(depth):
        wup = jnp.concatenate([w_lora_up[l], jnp.zeros((n_aaa, a_width), F32)], axis=0).astype(BF16)
        aup = jnp.concatenate([jnp.zeros((n_dec, a_width), F32), a_lora_up[l]], axis=0).astype(BF16)
        ya = _rwkv_call(x, norm_mix_g[l], w_in[l][:, :cb0], b_in[l][:cb0], mu_shift[l], w0[l], wup, a0[l], aup,
                        g_lora_up[l].astype(BF16), k_k[l], k_a[l], r_k[l], lnx_g[l], lnx_b[l], ts=ts)
        yb = _rglru_call(x, norm_mix_g[l], w_in[l][:, cb0:cb2], b_in[l][cb0:cb2], conv_b_w[l], conv_b_b[l],
                         w_rg_a[l], b_rg_a[l], w_rg_x[l], b_rg_x[l], lru_lambda[l], ts=ts)
        x = _merge_call(x, ya, yb, norm_mix_g[l], w_in[l][:, cb2:], b_in[l][cb2:], w_branch_a[l], w_branch_b[l],
                        w_mix_out[l], tm=tm)
        kv = _kv_call(mem, norm_mem_g[l], w_ckv[l])
        x = _attn_call(x, kv, norm_x_g[l], w_cq[l], w_co[l], tm=tm, heads=heads)
        last = l == depth - 1
        x = _ffn_call(x, norm_ffn_g[l], w_ffn_in[l], ffn_conv_w[l], ffn_conv_b[l], w_ffn_out[l], norm_final_g,
                      tm=_pick(S, 256), ck=256, final_norm=last)
    return x
```

```python
import functools
import math

import jax
import jax.numpy as jnp
from jax import lax
from jax.experimental import pallas as pl
from jax.experimental.pallas import tpu as pltpu

F32 = jnp.float32
BF16 = jnp.bfloat16

NORM_EPS = 1e-6
LNX_EPS = 64e-5
LRU_C = 8.0
HEAD = 64
CHUNK = 64
GROUP = 4
SUBLANES = 8
VMEM_LIMIT = 56 * 1024 * 1024


def _rms(x, g):
    return x * lax.rsqrt(jnp.mean(x * x, axis=-1, keepdims=True) + NORM_EPS) * g


def _sigmoid(x):
    return 1.0 / (1.0 + jnp.exp(-x))


def _gelu(x):
    c = math.sqrt(2.0 / math.pi)
    return 0.5 * x * (1.0 + jnp.tanh(c * (x + 0.044715 * (x * x * x))))


def _bdot(a, b):
    return jnp.dot(a.astype(BF16), b.astype(BF16), preferred_element_type=F32)


def _split2_dot_left(m, x):
    hi = x.astype(BF16)
    lo = (x - hi.astype(F32)).astype(BF16)
    return (jnp.dot(m, hi, preferred_element_type=F32)
            + jnp.dot(m, lo, preferred_element_type=F32))


def _const_spec(shape):
    n = len(shape)
    return pl.BlockSpec(shape, lambda *_: (0,) * n)


def _round_robin(gens):
    active = list(gens)
    while active:
        alive = []
        for gen in active:
            try:
                next(gen)
                alive.append(gen)
            except StopIteration:
                pass
        active = alive


def _rwkv_body(x_ref, g_ref, w_ref, b_ref, mu_ref, w0_ref, wup_ref, a0_ref, aup_ref, gup_ref,
               kk_ref, ka_ref, rk_ref, lg_ref, lb_ref, bdm_ref, bdf_ref, lvl_ref, eye_ref, base_ref,
               tri_ref, eyeg_ref, o_ref, carry_ref, p_ref, *, ts, sub, width):
    s = pl.program_id(1)
    c1, c2, c3 = width, 2 * width, 3 * width
    C = CHUNK
    GW = GROUP * HEAD
    groups = range(width // GW)
    nchunks = sub // C
    nsub = ts // sub

    @pl.when(s == 0)
    def _():
        carry_ref[...] = jnp.zeros_like(carry_ref)
        p_ref[...] = jnp.zeros_like(p_ref)

    bdm = bdm_ref[...]
    bdf = bdf_ref[...]
    eye_pk = eye_ref[...]
    base_pk = base_ref[...]
    eye_gw = eyeg_ref[...]
    nlev = lvl_ref.shape[0]
    row = lax.broadcasted_iota(jnp.int32, (sub, 1), 0)
    ti = lax.broadcasted_iota(jnp.int32, (C, GW), 0)
    si = lax.broadcasted_iota(jnp.int32, (C, GW), 1) % HEAD
    strict = ti > si
    incl = ti >= si

    def head_sums(t):
        return jnp.concatenate([jnp.dot(t[:, i * GW:(i + 1) * GW].astype(BF16), bdm, preferred_element_type=F32)
                                for i in groups], axis=1)

    def bd_of(t):
        return jnp.concatenate([t] * GROUP, axis=0) * bdm

    pro_out = {}
    last_rows = {}
    res = {}
    ys = {}
    state = {}

    def pro(u):
        h = _rms(x_ref[u * sub:(u + 1) * sub, :], g_ref[...]).astype(BF16)
        p = jnp.dot(h, w_ref[...], preferred_element_type=F32) + b_ref[...]
        yield
        first = carry_ref[...] if u == 0 else last_rows[u - 1]
        last_rows[u] = p[sub - 1:sub, :]
        prev = jnp.where(row == 0, first, pltpu.roll(p, 1, 0))
        p = p + (prev - p) * mu_ref[...]
        r = p[:, :c1]
        k = p[:, c1:c2]
        v = p[:, c2:c3]
        lora = p[:, c3:c3 + 128]
        pg = p[:, c3 + 128:]
        yield
        w_pre = w0_ref[...] + _bdot(jnp.tanh(lora), wup_ref[...])
        z = -w_pre
        w_log = -(jnp.maximum(z, 0.0) + jnp.log(1.0 + jnp.exp(-jnp.abs(z)))) - 0.5
        ew = jnp.exp(w_log)
        yield
        a = _sigmoid(a0_ref[...] + _bdot(lora, aup_ref[...]))
        g = _bdot(_sigmoid(pg), gup_ref[...])
        yield
        kk = k * kk_ref[...]
        k2 = k * (1.0 + (a - 1.0) * ka_ref[...])
        kk = kk / jnp.maximum(jnp.sqrt(head_sums(kk * kk)), 1e-12)
        yield
        bonus = head_sums(r * k2 * rk_ref[...]) * v
        yield
        cl = _split2_dot_left(tri_ref[...], ew)
        yield
        w_inv = jnp.exp(cl)
        at = -kk * jnp.exp(ew - cl)
        yield
        rt = r * jnp.exp(-cl)
        bt = (kk * a) * w_inv
        kt = k2 * w_inv
        pro_out[u] = (at, rt, bt, kt, v, cl, bonus, g)

    def indep(u, c, gi):
        at, rt, bt, kt, v, cl = pro_out[u][:6]
        rs = slice(c * C, (c + 1) * C)
        ls = slice(gi * GW, (gi + 1) * GW)
        a_c, r_c, b_c, k_c, v_c = at[rs, ls], rt[rs, ls], bt[rs, ls], kt[rs, ls], v[rs, ls]
        lhs = jnp.concatenate([a_c, r_c], axis=0).astype(BF16)
        rhs = jnp.concatenate([bd_of(b_c.astype(BF16)), bd_of(k_c.astype(BF16))], axis=0)
        A = lax.dot_general(lhs, rhs, (((1,), (1,)), ((), ())), preferred_element_type=F32)
        yield
        n = jnp.where(strict, A[:C, :GW], 0.0)
        aak = jnp.where(strict, A[:C, GW:], 0.0).astype(BF16)
        arbk = jnp.concatenate([jnp.where(incl, A[C:, :GW], 0.0), jnp.where(incl, A[C:, GW:], 0.0)],
                               axis=1).astype(BF16)
        nt = jnp.concatenate([n.astype(BF16)] * GROUP, axis=0)
        T = eye_pk + n * base_pk
        for lv in range(nlev):
            TM = jnp.dot(T.astype(BF16), nt * lvl_ref[lv], preferred_element_type=F32)
            yield
            T = T + jnp.dot(TM.astype(BF16), bd_of(T.astype(BF16)), preferred_element_type=F32)
            yield
        v_bf = v_c.astype(BF16)
        bdv = bd_of(v_bf)
        AV = jnp.dot(aak, bdv, preferred_element_type=F32)
        yield
        wc = jnp.exp(-cl[(c + 1) * C - 1:(c + 1) * C, ls])
        hat = jnp.concatenate([b_c * wc, k_c * wc], axis=0).astype(BF16)
        wcol = jnp.sum(eye_gw * wc, axis=1, keepdims=True)
        res[(u, c, gi)] = (lhs, T.astype(BF16), AV, arbk, bdv, hat, v_bf, wcol)

    def dep(u, gi):
        P = p_ref[gi] if u == 0 else state[gi]
        for c in range(nchunks):
            lhs, T_bf, AV, arbk, bdv, hat, v_bf, wcol = res[(u, c, gi)]
            AP = jnp.dot(lhs, P.astype(BF16), preferred_element_type=F32)
            yield
            U = jnp.dot(T_bf, bd_of((AP[:C] + AV).astype(BF16)), preferred_element_type=F32)
            yield
            U_bf = U.astype(BF16)
            ys[(u, c, gi)] = AP[C:] + jnp.dot(arbk, jnp.concatenate([bd_of(U_bf), bdv], axis=0),
                                              preferred_element_type=F32)
            upd = lax.dot_general(hat, jnp.concatenate([U_bf, v_bf], axis=0), (((0,), (0,)), ((), ())),
                                  preferred_element_type=F32)
            yield
            P = P * wcol + upd * bdf
        state[gi] = P
        if u == nsub - 1:
            p_ref[gi] = P

    def epi(u):
        bonus, g = pro_out[u][6:]
        y = jnp.concatenate([jnp.concatenate([ys[(u, c, gi)] for gi in groups], axis=1)
                             for c in range(nchunks)], axis=0)
        inv_n = 1.0 / HEAD
        mean = head_sums(y) * inv_n
        yield
        yc = y - mean
        var = head_sums(yc * yc) * inv_n
        yield
        yn = yc * lax.rsqrt(var + LNX_EPS) * lg_ref[...] + lb_ref[...]
        o_ref[u * sub:(u + 1) * sub, :] = ((yn + bonus) * g).astype(o_ref.dtype)

    for step in range(nsub + 3):
        gens = []
        if step < nsub:
            gens.append(pro(step))
        if 0 <= step - 1 < nsub:
            gens += [indep(step - 1, c, gi) for c in range(nchunks) for gi in groups]
        if 0 <= step - 2 < nsub:
            gens += [dep(step - 2, gi) for gi in groups]
        if 0 <= step - 3 < nsub:
            gens.append(epi(step - 3))
        _round_robin(gens)
    carry_ref[...] = last_rows[nsub - 1]


def _rwkv_call(x, g, w, b, mu, w0, wup, a0, aup, gup, k_k, k_a, r_k, lnx_g, lnx_b, *, ts, sub=256):
    sub = min(sub, ts)
    B, S, D = x.shape
    width = w0.shape[-1]
    cols = w.shape[-1]
    C, GW = CHUNK, GROUP * HEAD
    bdf = jnp.kron(jnp.eye(GROUP, dtype=F32), jnp.ones((HEAD, HEAD), F32))
    ti = jnp.arange(C)[:, None]
    si = jnp.arange(C)[None, :]
    levels = []
    m = 2
    while m < C:
        lm = ((ti // m) % 2 == 1) & ((si // m) % 2 == 0) & (ti // (2 * m) == si // (2 * m))
        levels.append(jnp.tile(lm.astype(F32), (GROUP, GROUP)) * bdf)
        m *= 2
    lvl = jnp.stack(levels).astype(BF16)
    eye_pk = jnp.tile(jnp.eye(C, dtype=F32), (1, GROUP))
    base_pk = jnp.tile(((ti // 2 == si // 2) & (ti > si)).astype(F32), (1, GROUP))
    tri = jnp.kron(jnp.eye(sub // C, dtype=F32), jnp.tril(jnp.ones((C, C), F32))).astype(BF16)
    row = lambda t: t.reshape(1, -1).astype(F32)
    consts = [row(g), w.astype(BF16), row(b), row(mu), row(w0), wup, row(a0), aup, gup,
              row(k_k), row(k_a), row(r_k), row(lnx_g), row(lnx_b),
              bdf.astype(BF16), bdf, lvl, eye_pk, base_pk, tri, jnp.eye(GW, dtype=F32)]
    body = functools.partial(_rwkv_body, ts=ts, sub=sub, width=width)
    return pl.pallas_call(
        body,
        grid=(B, S // ts),
        in_specs=[pl.BlockSpec((None, ts, D), lambda i, j: (i, j, 0))] + [_const_spec(c.shape) for c in consts],
        out_specs=pl.BlockSpec((None, ts, width), lambda i, j: (i, j, 0)),
        out_shape=jax.ShapeDtypeStruct((B, S, width), BF16),
        scratch_shapes=[pltpu.VMEM((1, cols), F32), pltpu.VMEM((width // GW, GW, GW), F32)],
        compiler_params=pltpu.CompilerParams(dimension_semantics=("arbitrary", "arbitrary"),
                                             vmem_limit_bytes=VMEM_LIMIT),
        name="rwkv",
    )(x, *consts)


def _rglru_body(x_ref, g_ref, w_ref, b_ref, cw_ref, cb_ref, wg_ref, bg_ref, lam_ref, o_ref,
                xs_ref, hc_ref, *, ts, width, kconv):
    s = pl.program_id(1)

    @pl.when(s == 0)
    def _():
        xs_ref[0:SUBLANES, :] = jnp.zeros((SUBLANES, width), F32)
        hc_ref[...] = jnp.zeros_like(hc_ref)

    h = _rms(x_ref[...], g_ref[...]).astype(BF16)
    pxy = jnp.dot(h, w_ref[...], preferred_element_type=F32) + b_ref[...]
    px = pxy[:, :width]
    py = pxy[:, width:]
    xs_ref[SUBLANES:SUBLANES + ts, :] = px
    xb = cb_ref[...]
    for j in range(kconv):
        xb = xb + xs_ref[pl.ds(SUBLANES - (kconv - 1) + j, ts), :] * cw_ref[j:j + 1, :]
    xs_ref[0:SUBLANES, :] = px[ts - SUBLANES:ts, :]

    gates = _bdot(xb, wg_ref[...]) + bg_ref[...]
    ga = _sigmoid(gates[:, :width])
    gx = _sigmoid(gates[:, width:])
    lam = lam_ref[...]
    log_sig = jnp.minimum(lam, 0.0) - jnp.log(1.0 + jnp.exp(-jnp.abs(lam)))
    log_a = LRU_C * ga * log_sig
    a = jnp.exp(log_a)
    mult = jnp.sqrt(1.0 - jnp.exp(2.0 * log_a))
    row = lax.broadcasted_iota(jnp.int32, (ts, 1), 0)
    mult = jnp.where((row == 0) & (s == 0), 1.0, mult)
    u = xb * gx * mult
    sub = row % SUBLANES
    d = 1
    while d < SUBLANES:
        keep = sub >= d
        u_sh = pltpu.roll(u, d, 0)
        a_sh = pltpu.roll(a, d, 0)
        u = u + jnp.where(keep, a * u_sh, 0.0)
        a = jnp.where(keep, a * a_sh, a)
        d *= 2
    h_prev = hc_ref[...]
    gelu_y = _gelu(py)
    for r0 in range(0, ts, 2 * SUBLANES):
        r1, r2 = r0 + SUBLANES, r0 + 2 * SUBLANES
        h0 = u[r0:r1, :] + a[r0:r1, :] * h_prev
        h1 = u[r1:r2, :] + a[r1:r2, :] * h0[SUBLANES - 1:SUBLANES, :]
        h_prev = h1[SUBLANES - 1:SUBLANES, :]
        hg = jnp.concatenate([h0, h1], axis=0)
        o_ref[r0:r2, :] = (hg * gelu_y[r0:r2, :]).astype(o_ref.dtype)
    hc_ref[...] = h_prev


def _rglru_call(x, g, w, b, conv_w, conv_b, w_ga, b_ga, w_gx, b_gx, lam, *, ts):
    B, S, D = x.shape
    width = lam.shape[-1]
    nb = w_ga.shape[0]
    bdiag = lambda t: jax.scipy.linalg.block_diag(*[t[i] for i in range(nb)])
    wg = jnp.concatenate([bdiag(w_ga), bdiag(w_gx)], axis=1).astype(BF16)
    row = lambda t: t.reshape(1, -1).astype(F32)
    consts = [row(g), w.astype(BF16), row(b), conv_w.astype(F32), row(conv_b), wg,
              jnp.concatenate([row(b_ga), row(b_gx)], axis=1), row(lam)]
    body = functools.partial(_rglru_body, ts=ts, width=width, kconv=conv_w.shape[0])
    return pl.pallas_call(
        body,
        grid=(B, S // ts),
        in_specs=[pl.BlockSpec((None, ts, D), lambda i, j: (i, j, 0))] + [_const_spec(c.shape) for c in consts],
        out_specs=pl.BlockSpec((None, ts, width), lambda i, j: (i, j, 0)),
        out_shape=jax.ShapeDtypeStruct((B, S, width), BF16),
        scratch_shapes=[pltpu.VMEM((ts + SUBLANES, width), F32), pltpu.VMEM((1, width), F32)],
        compiler_params=pltpu.CompilerParams(dimension_semantics=("arbitrary", "arbitrary"),
                                             vmem_limit_bytes=VMEM_LIMIT),
        name="rglru",
    )(x, *consts)


def _merge_body(x_ref, ya_ref, yb_ref, g_ref, wg_ref, bg_ref, wa_ref, wb_ref, wm_ref, o_ref, *, d):
    x = x_ref[...]
    h = _rms(x, g_ref[...]).astype(BF16)
    gates = jnp.dot(h, wg_ref[...], preferred_element_type=F32) + bg_ref[...]
    ya = jnp.dot(ya_ref[...], wa_ref[...], preferred_element_type=F32)
    yb = jnp.dot(yb_ref[...], wb_ref[...], preferred_element_type=F32)
    merged = _sigmoid(gates[:, :d]) * ya + _sigmoid(gates[:, d:]) * yb
    o_ref[...] = x + jnp.dot(merged.astype(BF16), wm_ref[...], preferred_element_type=F32)


def _merge_call(x, ya, yb, g, wg, bg, wa, wb, wm, *, tm):
    B, S, D = x.shape
    width = ya.shape[-1]
    consts = [g.reshape(1, -1).astype(F32), wg.astype(BF16), bg.reshape(1, -1).astype(F32),
              wa.astype(BF16), wb.astype(BF16), wm.astype(BF16)]
    tok = lambda n: pl.BlockSpec((None, tm, n), lambda i, j: (i, j, 0))
    return pl.pallas_call(
        functools.partial(_merge_body, d=D),
        grid=(B, S // tm),
        in_specs=[tok(D), tok(width), tok(width)] + [_const_spec(c.shape) for c in consts],
        out_specs=tok(D),
        out_shape=jax.ShapeDtypeStruct((B, S, D), F32),
        compiler_params=pltpu.CompilerParams(dimension_semantics=("arbitrary", "arbitrary"),
                                             vmem_limit_bytes=VMEM_LIMIT),
        name="merge",
    )(x, ya, yb, *consts)


def _kv_body(m_ref, g_ref, w_ref, o_ref):
    h = _rms(m_ref[...], g_ref[...]).astype(BF16)
    o_ref[...] = jnp.dot(h, w_ref[...], preferred_element_type=F32).astype(o_ref.dtype)


def _kv_call(mem, g, w):
    B, M, D = mem.shape
    consts = [g.reshape(1, -1).astype(F32), w.astype(BF16)]
    return pl.pallas_call(
        _kv_body,
        grid=(B,),
        in_specs=[pl.BlockSpec((None, M, D), lambda i: (i, 0, 0))] + [_const_spec(c.shape) for c in consts],
        out_specs=pl.BlockSpec((None, M, 2 * D), lambda i: (i, 0, 0)),
        out_shape=jax.ShapeDtypeStruct((B, M, 2 * D), BF16),
        compiler_params=pltpu.CompilerParams(dimension_semantics=("arbitrary",),
                                             vmem_limit_bytes=VMEM_LIMIT),
        name="memkv",
    )(mem, *consts)


def _attn_body(x_ref, kv_ref, g_ref, wq_ref, wo_ref, o_ref, *, d, heads):
    x = x_ref[...]
    h = _rms(x, g_ref[...]).astype(BF16)
    q = jnp.dot(h, wq_ref[...], preferred_element_type=F32)
    hd = d // heads
    scale = hd ** -0.5
    outs = []
    for i in range(heads):
        qh = q[:, i * hd:(i + 1) * hd].astype(BF16)
        kh = kv_ref[:, i * hd:(i + 1) * hd]
        vh = kv_ref[:, d + i * hd:d + (i + 1) * hd]
        sc = lax.dot_general(qh, kh, (((1,), (1,)), ((), ())), preferred_element_type=F32) * scale
        sc = sc - jnp.max(sc, axis=-1, keepdims=True)
        e = jnp.exp(sc)
        pr = e / jnp.sum(e, axis=-1, keepdims=True)
        outs.append(jnp.dot(pr.astype(BF16), vh, preferred_element_type=F32).astype(BF16))
    o = jnp.concatenate(outs, axis=1)
    o_ref[...] = x + jnp.dot(o, wo_ref[...], preferred_element_type=F32)


def _attn_call(x, kv, g, wq, wo, *, tm, heads):
    B, S, D = x.shape
    M = kv.shape[1]
    consts = [g.reshape(1, -1).astype(F32), wq.astype(BF16), wo.astype(BF16)]
    tok = pl.BlockSpec((None, tm, D), lambda i, j: (i, j, 0))
    return pl.pallas_call(
        functools.partial(_attn_body, d=D, heads=heads),
        grid=(B, S // tm),
        in_specs=[tok, pl.BlockSpec((None, M, 2 * D), lambda i, j: (i, 0, 0))]
        + [_const_spec(c.shape) for c in consts],
        out_specs=tok,
        out_shape=jax.ShapeDtypeStruct((B, S, D), F32),
        compiler_params=pltpu.CompilerParams(dimension_semantics=("arbitrary", "arbitrary"),
                                             vmem_limit_bytes=VMEM_LIMIT),
        name="attn",
    )(x, kv, *consts)


def _ffn_body(x_ref, g_ref, wi_ref, cw_ref, cb_ref, wo_ref, gf_ref, o_ref, us_ref, act_ref,
              *, tm, dff, ck, kconv, final_norm):
    s = pl.program_id(1)

    @pl.when(s == 0)
    def _():
        us_ref[0:SUBLANES, :] = jnp.zeros((SUBLANES, dff), F32)

    x = x_ref[...]
    h = _rms(x, g_ref[...]).astype(BF16)
    def proj_in(c0):
        return (jnp.dot(h, wi_ref[:, c0:c0 + ck], preferred_element_type=F32),
                jnp.dot(h, wi_ref[:, dff + c0:dff + c0 + ck], preferred_element_type=F32))

    starts = list(range(0, dff, ck))
    nxt = proj_in(starts[0])
    for i, c0 in enumerate(starts):
        ug, uv = nxt
        if i + 1 < len(starts):
            nxt = proj_in(starts[i + 1])
        us_ref[SUBLANES:SUBLANES + tm, c0:c0 + ck] = ug
        gate = cb_ref[:, c0:c0 + ck]
        for j in range(kconv):
            gate = gate + (us_ref[pl.ds(SUBLANES - (kconv - 1) + j, tm), c0:c0 + ck]
                           * cw_ref[j:j + 1, c0:c0 + ck])
        us_ref[0:SUBLANES, c0:c0 + ck] = ug[tm - SUBLANES:tm, :]
        act_ref[:, c0:c0 + ck] = (_gelu(gate) * uv).astype(BF16)
    acc = x + jnp.dot(act_ref[...], wo_ref[...], preferred_element_type=F32)
    if final_norm:
        acc = _rms(acc, gf_ref[...])
    o_ref[...] = acc


def _ffn_call(x, g, wi, conv_w, conv_b, wo, gf, *, tm, ck, final_norm):
    B, S, D = x.shape
    dff = wo.shape[0]
    consts = [g.reshape(1, -1).astype(F32), wi.astype(BF16), conv_w.astype(F32),
              conv_b.reshape(1, -1).astype(F32), wo.astype(BF16), gf.reshape(1, -1).astype(F32)]
    tok = pl.BlockSpec((None, tm, D), lambda i, j: (i, j, 0))
    body = functools.partial(_ffn_body, tm=tm, dff=dff, ck=ck, kconv=conv_w.shape[0], final_norm=final_norm)
    return pl.pallas_call(
        body,
        grid=(B, S // tm),
        in_specs=[tok] + [_const_spec(c.shape) for c in consts],
        out_specs=tok,
        out_shape=jax.ShapeDtypeStruct((B, S, D), F32),
        scratch_shapes=[pltpu.VMEM((tm + SUBLANES, dff), F32), pltpu.VMEM((tm, dff), BF16)],
        compiler_params=pltpu.CompilerParams(dimension_semantics=("arbitrary", "arbitrary"),
                                             vmem_limit_bytes=VMEM_LIMIT),
        name="ffn",
    )(x, *consts)


def _pick(n, pref):
    t = min(n, pref)
    while n % t:
        t //= 2
    return t


def kernel(x, mem, norm_mix_g, w_in, b_in, mu_shift, w0, w_lora_up, a0, a_lora_up, g_lora_up, k_k, k_a, r_k, lnx_g, lnx_b, w_branch_a, conv_b_w, conv_b_b, w_rg_a, b_rg_a, w_rg_x, b_rg_x, lru_lambda, w_branch_b, w_mix_out, norm_x_g, norm_mem_g, w_cq, w_ckv, w_co, norm_ffn_g, w_ffn_in, ffn_conv_w, ffn_conv_b, w_ffn_out, norm_final_g):
    depth = w_in.shape[0]
    B, S, D = x.shape
    a_width = w0.shape[-1]
    b_width = lru_lambda.shape[-1]
    n_dec, n_aaa = w_lora_up.shape[1], a_lora_up.shape[1]
    assert n_dec + n_aaa == 128, "decay/a lora widths must fill one 128-lane slab"
    cb0 = mu_shift.shape[-1]
    cb2 = cb0 + 2 * b_width
    heads = 4
    ts = _pick(S, 256)
    tm = _pick(S, 512)
    for l in range(depth):
        wup = jnp.concatenate([w_lora_up[l], jnp.zeros((n_aaa, a_width), F32)], axis=0).astype(BF16)
        aup = jnp.concatenate([jnp.zeros((n_dec, a_width), F32), a_lora_up[l]], axis=0).astype(BF16)
        ya = _rwkv_call(x, norm_mix_g[l], w_in[l][:, :cb0], b_in[l][:cb0], mu_shift[l], w0[l], wup, a0[l], aup,
                        g_lora_up[l].astype(BF16), k_k[l], k_a[l], r_k[l], lnx_g[l], lnx_b[l], ts=_pick(S, 512))
        yb = _rglru_call(x, norm_mix_g[l], w_in[l][:, cb0:cb2], b_in[l][cb0:cb2], conv_b_w[l], conv_b_b[l],
                         w_rg_a[l], b_rg_a[l], w_rg_x[l], b_rg_x[l], lru_lambda[l], ts=ts)
        x = _merge_call(x, ya, yb, norm_mix_g[l], w_in[l][:, cb2:], b_in[l][cb2:], w_branch_a[l], w_branch_b[l],
                        w_mix_out[l], tm=tm)
        kv = _kv_call(mem, norm_mem_g[l], w_ckv[l])
        x = _attn_call(x, kv, norm_x_g[l], w_cq[l], w_co[l], tm=tm, heads=heads)
        last = l == depth - 1
        x = _ffn_call(x, norm_ffn_g[l], w_ffn_in[l], ffn_conv_w[l], ffn_conv_b[l], w_ffn_out[l], norm_final_g,
                      tm=_pick(S, 256), ck=256, final_norm=last)
    return x
```

```python
import functools
import math

import jax
import jax.numpy as jnp
from jax import lax
from jax.experimental import pallas as pl
from jax.experimental.pallas import tpu as pltpu

F32 = jnp.float32
BF16 = jnp.bfloat16

NORM_EPS = 1e-6
LNX_EPS = 64e-5
LRU_C = 8.0
HEAD = 64
CHUNK = 64
GROUP = 4
SUBLANES = 8
VMEM_LIMIT = 56 * 1024 * 1024


def _rms(x, g):
    return x * lax.rsqrt(jnp.mean(x * x, axis=-1, keepdims=True) + NORM_EPS) * g


def _sigmoid(x):
    return 1.0 / (1.0 + jnp.exp(-x))


def _gelu(x):
    c = math.sqrt(2.0 / math.pi)
    return 0.5 * x * (1.0 + jnp.tanh(c * (x + 0.044715 * (x * x * x))))


def _sqrt(x):
    return jnp.where(x > 0.0, x * lax.rsqrt(x), 0.0)


def _group_roll(z, j):
    n, lanes = z.shape
    return pltpu.roll(z.reshape(n // SUBLANES, SUBLANES, lanes), j, 1).reshape(n, lanes)


def _shift_rows(x, prev8, j, sub8):
    r = _group_roll(jnp.concatenate([prev8, x], axis=0), j)
    return jnp.where(sub8 >= j, r[SUBLANES:], r[:-SUBLANES])


def _bdot(a, b):
    return jnp.dot(a.astype(BF16), b.astype(BF16), preferred_element_type=F32)


def _split2_dot_left(m, x):
    hi = x.astype(BF16)
    lo = (x - hi.astype(F32)).astype(BF16)
    return (jnp.dot(m, hi, preferred_element_type=F32)
            + jnp.dot(m, lo, preferred_element_type=F32))


def _const_spec(shape):
    n = len(shape)
    return pl.BlockSpec(shape, lambda *_: (0,) * n)


def _round_robin(gens):
    active = list(gens)
    while active:
        alive = []
        for gen in active:
            try:
                next(gen)
                alive.append(gen)
            except StopIteration:
                pass
        active = alive


def _rwkv_body(x_ref, g_ref, w_ref, b_ref, mu_ref, w0_ref, wup_ref, a0_ref, aup_ref, gup_ref,
               kk_ref, ka_ref, rk_ref, lg_ref, lb_ref, bdm_ref, bdf_ref, lvl_ref, eye_ref, base_ref,
               tri_ref, eyeg_ref, o_ref, carry_ref, p_ref, *, ts, sub, width):
    s = pl.program_id(1)
    c1, c2, c3 = width, 2 * width, 3 * width
    C = CHUNK
    GW = GROUP * HEAD
    groups = range(width // GW)
    nchunks = sub // C
    nsub = ts // sub

    @pl.when(s == 0)
    def _():
        carry_ref[...] = jnp.zeros_like(carry_ref)
        p_ref[...] = jnp.zeros_like(p_ref)

    bdm = bdm_ref[...]
    bdf = bdf_ref[...]
    eye_pk = eye_ref[...]
    base_pk = base_ref[...]
    eye_gw = eyeg_ref[...]
    nlev = lvl_ref.shape[0]
    sub8 = lax.broadcasted_iota(jnp.int32, (sub, 1), 0) % SUBLANES
    ti = lax.broadcasted_iota(jnp.int32, (C, GW), 0)
    si = lax.broadcasted_iota(jnp.int32, (C, GW), 1) % HEAD
    strict = ti > si
    incl = ti >= si

    def head_sums(t):
        return jnp.concatenate([jnp.dot(t[:, i * GW:(i + 1) * GW].astype(BF16), bdm, preferred_element_type=F32)
                                for i in groups], axis=1)

    def bd_of(t):
        return jnp.concatenate([t] * GROUP, axis=0) * bdm

    pro_out = {}
    last_rows = {}
    res = {}
    ys = {}
    state = {}

    def pro(u):
        h = _rms(x_ref[u * sub:(u + 1) * sub, :], g_ref[...]).astype(BF16)
        p = jnp.dot(h, w_ref[...], preferred_element_type=F32) + b_ref[...]
        yield
        first = carry_ref[...] if u == 0 else last_rows[u - 1]
        last_rows[u] = p[sub - SUBLANES:sub, :]
        prev = _shift_rows(p, first, 1, sub8)
        p = p + (prev - p) * mu_ref[...]
        r = p[:, :c1]
        k = p[:, c1:c2]
        v = p[:, c2:c3]
        lora = p[:, c3:c3 + 128]
        pg = p[:, c3 + 128:]
        yield
        w_pre = w0_ref[...] + _bdot(jnp.tanh(lora), wup_ref[...])
        z = -w_pre
        w_log = -(jnp.maximum(z, 0.0) + jnp.log(1.0 + jnp.exp(-jnp.abs(z)))) - 0.5
        ew = jnp.exp(w_log)
        yield
        a = _sigmoid(a0_ref[...] + _bdot(lora, aup_ref[...]))
        g = _bdot(_sigmoid(pg), gup_ref[...])
        yield
        kk = k * kk_ref[...]
        k2 = k * (1.0 + (a - 1.0) * ka_ref[...])
        kk = kk * jnp.minimum(lax.rsqrt(head_sums(kk * kk)), 1e12)
        yield
        bonus = head_sums(r * k2 * rk_ref[...]) * v
        yield
        cl = _split2_dot_left(tri_ref[...], ew)
        yield
        w_inv = jnp.exp(cl)
        at = -kk * jnp.exp(ew - cl)
        yield
        rt = r * jnp.exp(-cl)
        bt = (kk * a) * w_inv
        kt = k2 * w_inv
        pro_out[u] = (at, rt, bt, kt, v, cl, bonus, g)

    def indep(u, c, gi):
        at, rt, bt, kt, v, cl = pro_out[u][:6]
        rs = slice(c * C, (c + 1) * C)
        ls = slice(gi * GW, (gi + 1) * GW)
        a_c, r_c, b_c, k_c, v_c = at[rs, ls], rt[rs, ls], bt[rs, ls], kt[rs, ls], v[rs, ls]
        lhs = jnp.concatenate([a_c, r_c], axis=0).astype(BF16)
        rhs = jnp.concatenate([bd_of(b_c.astype(BF16)), bd_of(k_c.astype(BF16))], axis=0)
        A = lax.dot_general(lhs, rhs, (((1,), (1,)), ((), ())), preferred_element_type=F32)
        yield
        n = jnp.where(strict, A[:C, :GW], 0.0)
        aak = jnp.where(strict, A[:C, GW:], 0.0).astype(BF16)
        arbk = jnp.concatenate([jnp.where(incl, A[C:, :GW], 0.0), jnp.where(incl, A[C:, GW:], 0.0)],
                               axis=1).astype(BF16)
        nt = jnp.concatenate([n.astype(BF16)] * GROUP, axis=0)
        T = eye_pk + n * base_pk
        for lv in range(nlev):
            TM = jnp.dot(T.astype(BF16), nt * lvl_ref[lv], preferred_element_type=F32)
            yield
            T = T + jnp.dot(TM.astype(BF16), bd_of(T.astype(BF16)), preferred_element_type=F32)
            yield
        v_bf = v_c.astype(BF16)
        bdv = bd_of(v_bf)
        AV = jnp.dot(aak, bdv, preferred_element_type=F32)
        yield
        wc = jnp.exp(-cl[(c + 1) * C - 1:(c + 1) * C, ls])
        hat = jnp.concatenate([b_c * wc, k_c * wc], axis=0).astype(BF16)
        wcol = jnp.sum(eye_gw * wc, axis=1, keepdims=True)
        res[(u, c, gi)] = (lhs, T.astype(BF16), AV, arbk, bdv, hat, v_bf, wcol)

    def dep(u, gi):
        P = p_ref[gi] if u == 0 else state[gi]
        for c in range(nchunks):
            lhs, T_bf, AV, arbk, bdv, hat, v_bf, wcol = res[(u, c, gi)]
            AP = jnp.dot(lhs, P.astype(BF16), preferred_element_type=F32)
            yield
            U = jnp.dot(T_bf, bd_of((AP[:C] + AV).astype(BF16)), preferred_element_type=F32)
            yield
            U_bf = U.astype(BF16)
            ys[(u, c, gi)] = AP[C:] + jnp.dot(arbk, jnp.concatenate([bd_of(U_bf), bdv], axis=0),
                                              preferred_element_type=F32)
            upd = lax.dot_general(hat, jnp.concatenate([U_bf, v_bf], axis=0), (((0,), (0,)), ((), ())),
                                  preferred_element_type=F32)
            yield
            P = P * wcol + upd * bdf
        state[gi] = P
        if u == nsub - 1:
            p_ref[gi] = P

    def epi(u):
        bonus, g = pro_out[u][6:]
        y = jnp.concatenate([jnp.concatenate([ys[(u, c, gi)] for gi in groups], axis=1)
                             for c in range(nchunks)], axis=0)
        inv_n = 1.0 / HEAD
        mean = head_sums(y) * inv_n
        yield
        yc = y - mean
        var = head_sums(yc * yc) * inv_n
        yield
        yn = yc * lax.rsqrt(var + LNX_EPS) * lg_ref[...] + lb_ref[...]
        o_ref[u * sub:(u + 1) * sub, :] = ((yn + bonus) * g).astype(o_ref.dtype)

    for step in range(nsub + 3):
        gens = []
        if step < nsub:
            gens.append(pro(step))
        if 0 <= step - 1 < nsub:
            gens += [indep(step - 1, c, gi) for c in range(nchunks) for gi in groups]
        if 0 <= step - 2 < nsub:
            gens += [dep(step - 2, gi) for gi in groups]
        if 0 <= step - 3 < nsub:
            gens.append(epi(step - 3))
        _round_robin(gens)
    carry_ref[...] = last_rows[nsub - 1]


def _rwkv_call(x, g, w, b, mu, w0, wup, a0, aup, gup, k_k, k_a, r_k, lnx_g, lnx_b, *, ts, sub=256):
    sub = min(sub, ts)
    B, S, D = x.shape
    width = w0.shape[-1]
    cols = w.shape[-1]
    C, GW = CHUNK, GROUP * HEAD
    bdf = jnp.kron(jnp.eye(GROUP, dtype=F32), jnp.ones((HEAD, HEAD), F32))
    ti = jnp.arange(C)[:, None]
    si = jnp.arange(C)[None, :]
    levels = []
    m = 2
    while m < C:
        lm = ((ti // m) % 2 == 1) & ((si // m) % 2 == 0) & (ti // (2 * m) == si // (2 * m))
        levels.append(jnp.tile(lm.astype(F32), (GROUP, GROUP)) * bdf)
        m *= 2
    lvl = jnp.stack(levels).astype(BF16)
    eye_pk = jnp.tile(jnp.eye(C, dtype=F32), (1, GROUP))
    base_pk = jnp.tile(((ti // 2 == si // 2) & (ti > si)).astype(F32), (1, GROUP))
    tri = jnp.kron(jnp.eye(sub // C, dtype=F32), jnp.tril(jnp.ones((C, C), F32))).astype(BF16)
    row = lambda t: t.reshape(1, -1).astype(F32)
    consts = [row(g), w.astype(BF16), row(b), row(mu), row(w0), wup, row(a0), aup, gup,
              row(k_k), row(k_a), row(r_k), row(lnx_g), row(lnx_b),
              bdf.astype(BF16), bdf, lvl, eye_pk, base_pk, tri, jnp.eye(GW, dtype=F32)]
    body = functools.partial(_rwkv_body, ts=ts, sub=sub, width=width)
    return pl.pallas_call(
        body,
        grid=(B, S // ts),
        in_specs=[pl.BlockSpec((None, ts, D), lambda i, j: (i, j, 0))] + [_const_spec(c.shape) for c in consts],
        out_specs=pl.BlockSpec((None, ts, width), lambda i, j: (i, j, 0)),
        out_shape=jax.ShapeDtypeStruct((B, S, width), BF16),
        scratch_shapes=[pltpu.VMEM((SUBLANES, cols), F32), pltpu.VMEM((width // GW, GW, GW), F32)],
        compiler_params=pltpu.CompilerParams(dimension_semantics=("arbitrary", "arbitrary"),
                                             vmem_limit_bytes=VMEM_LIMIT),
        name="rwkv",
    )(x, *consts)


def _mix_body(x_ref, ya_ref, g_ref, wr_ref, br_ref, cw_ref, cb_ref, wgr_ref, bgr_ref, lam_ref,
              wg_ref, bg_ref, wa_ref, wb_ref, wm_ref, o_ref, xs_ref, hc_ref,
              *, ts, sub, width, d, kconv):
    s = pl.program_id(1)
    nsub = ts // sub

    @pl.when(s == 0)
    def _():
        xs_ref[...] = jnp.zeros_like(xs_ref)
        hc_ref[...] = jnp.zeros_like(hc_ref)

    row = lax.broadcasted_iota(jnp.int32, (sub, 1), 0)
    sub8 = row % SUBLANES
    lam = lam_ref[...]
    log_sig = jnp.minimum(lam, 0.0) - jnp.log(1.0 + jnp.exp(-jnp.abs(lam)))
    hs, ybs, heads_out, carries, tails = {}, {}, {}, {}, {}

    def rg(u):
        x = x_ref[u * sub:(u + 1) * sub, :]
        h = _rms(x, g_ref[...]).astype(BF16)
        hs[u] = h
        yield
        pxy = jnp.dot(h, wr_ref[...], preferred_element_type=F32) + br_ref[...]
        px = pxy[:, :width]
        py = pxy[:, width:]
        yield
        prev8 = xs_ref[...] if u == 0 else tails[u - 1]
        tails[u] = px[sub - SUBLANES:sub, :]
        xb = cb_ref[...] + px * cw_ref[kconv - 1:kconv, :]
        for j in range(1, kconv):
            xb = xb + _shift_rows(px, prev8, j, sub8) * cw_ref[kconv - 1 - j:kconv - j, :]
        yield
        gates = _bdot(xb, wgr_ref[...]) + bgr_ref[...]
        ga = _sigmoid(gates[:, :width])
        gx = _sigmoid(gates[:, width:])
        yield
        log_a = LRU_C * ga * log_sig
        a = jnp.exp(log_a)
        mult = _sqrt(1.0 - jnp.exp(2.0 * log_a))
        if u == 0:
            mult = jnp.where((row == 0) & (s == 0), 1.0, mult)
        uu = xb * gx * mult
        yield
        dd = 1
        while dd < SUBLANES:
            keep = sub8 >= dd
            u_sh = _group_roll(uu, dd)
            a_sh = _group_roll(a, dd)
            uu = uu + jnp.where(keep, a * u_sh, 0.0)
            a = jnp.where(keep, a * a_sh, a)
            dd *= 2
            yield
        h_prev = hc_ref[...] if u == 0 else carries[u - 1]
        gelu_y = _gelu(py)
        yield
        outs = []
        for r0 in range(0, sub, SUBLANES):
            hg = uu[r0:r0 + SUBLANES, :] + a[r0:r0 + SUBLANES, :] * h_prev
            h_prev = hg[SUBLANES - 1:SUBLANES, :]
            outs.append(hg)
        carries[u] = h_prev
        ybs[u] = (jnp.concatenate(outs, axis=0) * gelu_y).astype(BF16)

    def head(u):
        yield
        h = hs[u]
        half = d // 2
        parts = []
        for c0 in range(0, 2 * d, half):
            parts.append(jnp.dot(h, wg_ref[:, c0:c0 + half], preferred_element_type=F32) + bg_ref[:, c0:c0 + half])
            yield
        ya = jnp.dot(ya_ref[u * sub:(u + 1) * sub, :], wa_ref[...], preferred_element_type=F32)
        heads_out[u] = (jnp.concatenate(parts[:2], axis=1), jnp.concatenate(parts[2:], axis=1), ya)

    def tail(u):
        ga, gb, ya = heads_out[u]
        yb = jnp.dot(ybs[u], wb_ref[...], preferred_element_type=F32)
        yield
        merged = (_sigmoid(ga) * ya + _sigmoid(gb) * yb).astype(BF16)
        yield
        o_ref[u * sub:(u + 1) * sub, :] = (x_ref[u * sub:(u + 1) * sub, :]
                                           + jnp.dot(merged, wm_ref[...], preferred_element_type=F32))

    for step in range(nsub + 1):
        gens = []
        if step < nsub:
            gens += [rg(step), head(step)]
        if step >= 1:
            gens.append(tail(step - 1))
        _round_robin(gens)
    hc_ref[...] = carries[nsub - 1]
    xs_ref[...] = tails[nsub - 1]


def _mix_call(x, ya, g, wr, br, conv_w, conv_b, w_ga, b_ga, w_gx, b_gx, lam, wg, bg, wa, wb, wm, *, ts, sub=256):
    sub = min(sub, ts)
    B, S, D = x.shape
    width = lam.shape[-1]
    nb = w_ga.shape[0]
    bdiag = lambda t: jax.scipy.linalg.block_diag(*[t[i] for i in range(nb)])
    wgr = jnp.concatenate([bdiag(w_ga), bdiag(w_gx)], axis=1).astype(BF16)
    row = lambda t: t.reshape(1, -1).astype(F32)
    consts = [row(g), wr.astype(BF16), row(br), conv_w.astype(F32), row(conv_b), wgr,
              jnp.concatenate([row(b_ga), row(b_gx)], axis=1), row(lam),
              wg.astype(BF16), row(bg), wa.astype(BF16), wb.astype(BF16), wm.astype(BF16)]
    body = functools.partial(_mix_body, ts=ts, sub=sub, width=width, d=D, kconv=conv_w.shape[0])
    tok = lambda n: pl.BlockSpec((None, ts, n), lambda i, j: (i, j, 0))
    return pl.pallas_call(
        body,
        grid=(B, S // ts),
        in_specs=[tok(D), tok(ya.shape[-1])] + [_const_spec(c.shape) for c in consts],
        out_specs=tok(D),
        out_shape=jax.ShapeDtypeStruct((B, S, D), F32),
        scratch_shapes=[pltpu.VMEM((SUBLANES, width), F32), pltpu.VMEM((1, width), F32)],
        compiler_params=pltpu.CompilerParams(dimension_semantics=("arbitrary", "arbitrary"),
                                             vmem_limit_bytes=VMEM_LIMIT),
        name="mix",
    )(x, ya, *consts)


def _kv_body(m_ref, g_ref, w_ref, o_ref):
    h = _rms(m_ref[...], g_ref[...]).astype(BF16)
    o_ref[...] = jnp.dot(h, w_ref[...], preferred_element_type=F32).astype(o_ref.dtype)


def _kv_call(mem, g, w):
    B, M, D = mem.shape
    consts = [g.reshape(1, -1).astype(F32), w.astype(BF16)]
    return pl.pallas_call(
        _kv_body,
        grid=(B,),
        in_specs=[pl.BlockSpec((None, M, D), lambda i: (i, 0, 0))] + [_const_spec(c.shape) for c in consts],
        out_specs=pl.BlockSpec((None, M, 2 * D), lambda i: (i, 0, 0)),
        out_shape=jax.ShapeDtypeStruct((B, M, 2 * D), BF16),
        compiler_params=pltpu.CompilerParams(dimension_semantics=("arbitrary",),
                                             vmem_limit_bytes=VMEM_LIMIT),
        name="memkv",
    )(mem, *consts)


def _attn_body(x_ref, kv_ref, g_ref, wq_ref, wo_ref, o_ref, *, d, heads):
    x = x_ref[...]
    h = _rms(x, g_ref[...]).astype(BF16)
    q = jnp.dot(h, wq_ref[...], preferred_element_type=F32)
    hd = d // heads
    scale = hd ** -0.5
    outs = []
    for i in range(heads):
        qh = q[:, i * hd:(i + 1) * hd].astype(BF16)
        kh = kv_ref[:, i * hd:(i + 1) * hd]
        vh = kv_ref[:, d + i * hd:d + (i + 1) * hd]
        sc = lax.dot_general(qh, kh, (((1,), (1,)), ((), ())), preferred_element_type=F32) * scale
        sc = sc - jnp.max(sc, axis=-1, keepdims=True)
        e = jnp.exp(sc)
        pr = e / jnp.sum(e, axis=-1, keepdims=True)
        outs.append(jnp.dot(pr.astype(BF16), vh, preferred_element_type=F32).astype(BF16))
    o = jnp.concatenate(outs, axis=1)
    o_ref[...] = x + jnp.dot(o, wo_ref[...], preferred_element_type=F32)


def _attn_call(x, kv, g, wq, wo, *, tm, heads):
    B, S, D = x.shape
    M = kv.shape[1]
    consts = [g.reshape(1, -1).astype(F32), wq.astype(BF16), wo.astype(BF16)]
    tok = pl.BlockSpec((None, tm, D), lambda i, j: (i, j, 0))
    return pl.pallas_call(
        functools.partial(_attn_body, d=D, heads=heads),
        grid=(B, S // tm),
        in_specs=[tok, pl.BlockSpec((None, M, 2 * D), lambda i, j: (i, 0, 0))]
        + [_const_spec(c.shape) for c in consts],
        out_specs=tok,
        out_shape=jax.ShapeDtypeStruct((B, S, D), F32),
        compiler_params=pltpu.CompilerParams(dimension_semantics=("arbitrary", "arbitrary"),
                                             vmem_limit_bytes=VMEM_LIMIT),
        name="attn",
    )(x, kv, *consts)


def _ffn_body(x_ref, g_ref, wi_ref, cw_ref, cb_ref, wo_ref, gf_ref, o_ref, us_ref, act_ref,
              *, tm, dff, ck, kconv, final_norm):
    s = pl.program_id(1)

    @pl.when(s == 0)
    def _():
        us_ref[0:SUBLANES, :] = jnp.zeros((SUBLANES, dff), F32)

    x = x_ref[...]
    h = _rms(x, g_ref[...]).astype(BF16)
    def proj_in(c0):
        return (jnp.dot(h, wi_ref[:, c0:c0 + ck], preferred_element_type=F32),
                jnp.dot(h, wi_ref[:, dff + c0:dff + c0 + ck], preferred_element_type=F32))

    starts = list(range(0, dff, ck))
    nxt = proj_in(starts[0])
    for i, c0 in enumerate(starts):
        ug, uv = nxt
        if i + 1 < len(starts):
            nxt = proj_in(starts[i + 1])
        us_ref[SUBLANES:SUBLANES + tm, c0:c0 + ck] = ug
        gate = cb_ref[:, c0:c0 + ck]
        for j in range(kconv):
            gate = gate + (us_ref[pl.ds(SUBLANES - (kconv - 1) + j, tm), c0:c0 + ck]
                           * cw_ref[j:j + 1, c0:c0 + ck])
        us_ref[0:SUBLANES, c0:c0 + ck] = ug[tm - SUBLANES:tm, :]
        act_ref[:, c0:c0 + ck] = (_gelu(gate) * uv).astype(BF16)
    acc = x + jnp.dot(act_ref[...], wo_ref[...], preferred_element_type=F32)
    if final_norm:
        acc = _rms(acc, gf_ref[...])
    o_ref[...] = acc


def _ffn_call(x, g, wi, conv_w, conv_b, wo, gf, *, tm, ck, final_norm):
    B, S, D = x.shape
    dff = wo.shape[0]
    consts = [g.reshape(1, -1).astype(F32), wi.astype(BF16), conv_w.astype(F32),
              conv_b.reshape(1, -1).astype(F32), wo.astype(BF16), gf.reshape(1, -1).astype(F32)]
    tok = pl.BlockSpec((None, tm, D), lambda i, j: (i, j, 0))
    body = functools.partial(_ffn_body, tm=tm, dff=dff, ck=ck, kconv=conv_w.shape[0], final_norm=final_norm)
    return pl.pallas_call(
        body,
        grid=(B, S // tm),
        in_specs=[tok] + [_const_spec(c.shape) for c in consts],
        out_specs=tok,
        out_shape=jax.ShapeDtypeStruct((B, S, D), F32),
        scratch_shapes=[pltpu.VMEM((tm + SUBLANES, dff), F32), pltpu.VMEM((tm, dff), BF16)],
        compiler_params=pltpu.CompilerParams(dimension_semantics=("arbitrary", "arbitrary"),
                                             vmem_limit_bytes=VMEM_LIMIT),
        name="ffn",
    )(x, *consts)


def _pick(n, pref):
    t = min(n, pref)
    while n % t:
        t //= 2
    return t


def kernel(x, mem, norm_mix_g, w_in, b_in, mu_shift, w0, w_lora_up, a0, a_lora_up, g_lora_up, k_k, k_a, r_k, lnx_g, lnx_b, w_branch_a, conv_b_w, conv_b_b, w_rg_a, b_rg_a, w_rg_x, b_rg_x, lru_lambda, w_branch_b, w_mix_out, norm_x_g, norm_mem_g, w_cq, w_ckv, w_co, norm_ffn_g, w_ffn_in, ffn_conv_w, ffn_conv_b, w_ffn_out, norm_final_g):
    depth = w_in.shape[0]
    B, S, D = x.shape
    a_width = w0.shape[-1]
    b_width = lru_lambda.shape[-1]
    n_dec, n_aaa = w_lora_up.shape[1], a_lora_up.shape[1]
    assert n_dec + n_aaa == 128, "decay/a lora widths must fill one 128-lane slab"
    cb0 = mu_shift.shape[-1]
    cb2 = cb0 + 2 * b_width
    heads = 4
    tm = _pick(S, 512)
    for l in range(depth):
        wup = jnp.concatenate([w_lora_up[l], jnp.zeros((n_aaa, a_width), F32)], axis=0).astype(BF16)
        aup = jnp.concatenate([jnp.zeros((n_dec, a_width), F32), a_lora_up[l]], axis=0).astype(BF16)
        ya = _rwkv_call(x, norm_mix_g[l], w_in[l][:, :cb0], b_in[l][:cb0], mu_shift[l], w0[l], wup, a0[l], aup,
                        g_lora_up[l].astype(BF16), k_k[l], k_a[l], r_k[l], lnx_g[l], lnx_b[l], ts=_pick(S, 512))
        x = _mix_call(x, ya, norm_mix_g[l], w_in[l][:, cb0:cb2], b_in[l][cb0:cb2], conv_b_w[l], conv_b_b[l],
                      w_rg_a[l], b_rg_a[l], w_rg_x[l], b_rg_x[l], lru_lambda[l], w_in[l][:, cb2:], b_in[l][cb2:],
                      w_branch_a[l], w_branch_b[l], w_mix_out[l], ts=_pick(S, 512))
        kv = _kv_call(mem, norm_mem_g[l], w_ckv[l])
        x = _attn_call(x, kv, norm_x_g[l], w_cq[l], w_co[l], tm=tm, heads=heads)
        last = l == depth - 1
        x = _ffn_call(x, norm_ffn_g[l], w_ffn_in[l], ffn_conv_w[l], ffn_conv_b[l], w_ffn_out[l], norm_final_g,
                      tm=_pick(S, 512), ck=256, final_norm=last)
    return x
```

```python
import functools
import math

import jax
import jax.numpy as jnp
from jax import lax
from jax.experimental import pallas as pl
from jax.experimental.pallas import tpu as pltpu

F32 = jnp.float32
BF16 = jnp.bfloat16

NORM_EPS = 1e-6
LNX_EPS = 64e-5
LRU_C = 8.0
HEAD = 64
CHUNK = 64
GROUP = 4
SUBLANES = 8
VMEM_LIMIT = 56 * 1024 * 1024


def _rms(x, g):
    return x * lax.rsqrt(jnp.mean(x * x, axis=-1, keepdims=True) + NORM_EPS) * g


def _sigmoid(x):
    return 1.0 / (1.0 + jnp.exp(-x))


def _gelu(x):
    c = math.sqrt(2.0 / math.pi)
    return 0.5 * x * (1.0 + jnp.tanh(c * (x + 0.044715 * (x * x * x))))


def _sqrt(x):
    return jnp.where(x > 0.0, x * lax.rsqrt(x), 0.0)


def _group_roll(z, j):
    n, lanes = z.shape
    return pltpu.roll(z.reshape(n // SUBLANES, SUBLANES, lanes), j, 1).reshape(n, lanes)


def _shift_rows(x, prev8, j, sub8):
    r = _group_roll(jnp.concatenate([prev8, x], axis=0), j)
    return jnp.where(sub8 >= j, r[SUBLANES:], r[:-SUBLANES])


def _bdot(a, b):
    return jnp.dot(a.astype(BF16), b.astype(BF16), preferred_element_type=F32)


def _split2_dot_left(m, x):
    hi = x.astype(BF16)
    lo = (x - hi.astype(F32)).astype(BF16)
    return (jnp.dot(m, hi, preferred_element_type=F32)
            + jnp.dot(m, lo, preferred_element_type=F32))


def _const_spec(shape):
    n = len(shape)
    return pl.BlockSpec(shape, lambda *_: (0,) * n)


def _round_robin(gens):
    active = list(gens)
    while active:
        alive = []
        for gen in active:
            try:
                next(gen)
                alive.append(gen)
            except StopIteration:
                pass
        active = alive


def _rwkv_body(x_ref, g_ref, w_ref, b_ref, mu_ref, w0_ref, wup_ref, a0_ref, aup_ref, gup_ref,
               kk_ref, ka_ref, rk_ref, lg_ref, lb_ref, bdm_ref, bdf_ref, lvl_ref, halfm_ref, eye_ref, base_ref,
               tri_ref, eyeg_ref, o_ref, carry_ref, p_ref, *, ts, sub, width):
    s = pl.program_id(1)
    c1, c2, c3 = width, 2 * width, 3 * width
    C = CHUNK
    GW = GROUP * HEAD
    groups = range(width // GW)
    nchunks = sub // C
    nsub = ts // sub

    @pl.when(s == 0)
    def _():
        carry_ref[...] = jnp.zeros_like(carry_ref)
        p_ref[...] = jnp.zeros_like(p_ref)

    bdm = bdm_ref[...]
    bdf = bdf_ref[...]
    eye_pk = eye_ref[...]
    base_pk = base_ref[...]
    eye_gw = eyeg_ref[...]
    nlev = lvl_ref.shape[0]
    sub8 = lax.broadcasted_iota(jnp.int32, (sub, 1), 0) % SUBLANES
    ti = lax.broadcasted_iota(jnp.int32, (C, GW), 0)
    si = lax.broadcasted_iota(jnp.int32, (C, GW), 1) % HEAD
    strict = ti > si
    incl = ti >= si

    def head_sums(t):
        return jnp.concatenate([jnp.dot(t[:, i * GW:(i + 1) * GW].astype(BF16), bdm, preferred_element_type=F32)
                                for i in groups], axis=1)

    def bd_build(t, m):
        zero = jnp.zeros((C, 128), BF16)
        blocks = []
        for hd in range(GROUP):
            col = hd * HEAD // 128
            cols = [zero] * (GW // 128)
            cols[col] = t[:, col * 128:(col + 1) * 128] * m[hd * HEAD % 128 // HEAD]
            blocks.append(jnp.concatenate(cols, axis=1))
        return jnp.concatenate(blocks, axis=0)

    halfm = (halfm_ref[0], halfm_ref[1])

    def bd_of(t):
        return bd_build(t, halfm)

    pro_out = {}
    last_rows = {}
    res = {}
    ys = {}
    state = {}

    def pro(u):
        h = _rms(x_ref[u * sub:(u + 1) * sub, :], g_ref[...]).astype(BF16)
        p = jnp.dot(h, w_ref[...], preferred_element_type=F32) + b_ref[...]
        yield
        first = carry_ref[...] if u == 0 else last_rows[u - 1]
        last_rows[u] = p[sub - SUBLANES:sub, :]
        prev = _shift_rows(p, first, 1, sub8)
        p = p + (prev - p) * mu_ref[...]
        r = p[:, :c1]
        k = p[:, c1:c2]
        v = p[:, c2:c3]
        lora = p[:, c3:c3 + 128]
        pg = p[:, c3 + 128:]
        yield
        w_pre = w0_ref[...] + _bdot(jnp.tanh(lora), wup_ref[...])
        z = -w_pre
        w_log = -(jnp.maximum(z, 0.0) + jnp.log(1.0 + jnp.exp(-jnp.abs(z)))) - 0.5
        ew = jnp.exp(w_log)
        yield
        a = _sigmoid(a0_ref[...] + _bdot(lora, aup_ref[...]))
        g = _bdot(_sigmoid(pg), gup_ref[...])
        yield
        kk = k * kk_ref[...]
        k2 = k * (1.0 + (a - 1.0) * ka_ref[...])
        kk = kk * jnp.minimum(lax.rsqrt(head_sums(kk * kk)), 1e12)
        yield
        bonus = head_sums(r * k2 * rk_ref[...]) * v
        yield
        cl = _split2_dot_left(tri_ref[...], ew)
        yield
        w_inv = jnp.exp(cl)
        at = -kk * jnp.exp(ew - cl)
        yield
        rt = r * jnp.exp(-cl)
        bt = (kk * a) * w_inv
        kt = k2 * w_inv
        pro_out[u] = (at, rt, bt, kt, v, cl, bonus, g)

    def indep(u, c, gi):
        at, rt, bt, kt, v, cl = pro_out[u][:6]
        rs = slice(c * C, (c + 1) * C)
        ls = slice(gi * GW, (gi + 1) * GW)
        a_c, r_c, b_c, k_c, v_c = at[rs, ls], rt[rs, ls], bt[rs, ls], kt[rs, ls], v[rs, ls]
        lhs = jnp.concatenate([a_c, r_c], axis=0).astype(BF16)
        rhs = jnp.concatenate([bd_of(b_c.astype(BF16)), bd_of(k_c.astype(BF16))], axis=0)
        A = lax.dot_general(lhs, rhs, (((1,), (1,)), ((), ())), preferred_element_type=F32)
        yield
        n = jnp.where(strict, A[:C, :GW], 0.0)
        aak = jnp.where(strict, A[:C, GW:], 0.0).astype(BF16)
        arbk = jnp.concatenate([jnp.where(incl, A[C:, :GW], 0.0), jnp.where(incl, A[C:, GW:], 0.0)],
                               axis=1).astype(BF16)
        n_bf = n.astype(BF16)
        T = eye_pk + n * base_pk
        for lv in range(nlev):
            TM = jnp.dot(T.astype(BF16), bd_build(n_bf, (lvl_ref[lv, 0], lvl_ref[lv, 1])),
                         preferred_element_type=F32)
            yield
            T = T + jnp.dot(TM.astype(BF16), bd_of(T.astype(BF16)), preferred_element_type=F32)
            yield
        v_bf = v_c.astype(BF16)
        bdv = bd_of(v_bf)
        AV = jnp.dot(aak, bdv, preferred_element_type=F32)
        yield
        wc = jnp.exp(-cl[(c + 1) * C - 1:(c + 1) * C, ls])
        hat = jnp.concatenate([b_c * wc, k_c * wc], axis=0).astype(BF16)
        wcol = jnp.sum(eye_gw * wc, axis=1, keepdims=True)
        res[(u, c, gi)] = (lhs, T.astype(BF16), AV, arbk, bdv, hat, v_bf, wcol)

    def dep(u, gi):
        P = p_ref[gi] if u == 0 else state[gi]
        for c in range(nchunks):
            lhs, T_bf, AV, arbk, bdv, hat, v_bf, wcol = res[(u, c, gi)]
            AP = jnp.dot(lhs, P.astype(BF16), preferred_element_type=F32)
            yield
            U = jnp.dot(T_bf, bd_of((AP[:C] + AV).astype(BF16)), preferred_element_type=F32)
            yield
            U_bf = U.astype(BF16)
            ys[(u, c, gi)] = AP[C:] + jnp.dot(arbk, jnp.concatenate([bd_of(U_bf), bdv], axis=0),
                                              preferred_element_type=F32)
            upd = lax.dot_general(hat, jnp.concatenate([U_bf, v_bf], axis=0), (((0,), (0,)), ((), ())),
                                  preferred_element_type=F32)
            yield
            P = P * wcol + upd * bdf
        state[gi] = P
        if u == nsub - 1:
            p_ref[gi] = P

    def epi(u):
        bonus, g = pro_out[u][6:]
        y = jnp.concatenate([jnp.concatenate([ys[(u, c, gi)] for gi in groups], axis=1)
                             for c in range(nchunks)], axis=0)
        inv_n = 1.0 / HEAD
        mean = head_sums(y) * inv_n
        yield
        yc = y - mean
        var = head_sums(yc * yc) * inv_n
        yield
        yn = yc * lax.rsqrt(var + LNX_EPS) * lg_ref[...] + lb_ref[...]
        o_ref[u * sub:(u + 1) * sub, :] = ((yn + bonus) * g).astype(o_ref.dtype)

    for step in range(nsub + 3):
        gens = []
        if step < nsub:
            gens.append(pro(step))
        if 0 <= step - 1 < nsub:
            gens += [indep(step - 1, c, gi) for c in range(nchunks) for gi in groups]
        if 0 <= step - 2 < nsub:
            gens += [dep(step - 2, gi) for gi in groups]
        if 0 <= step - 3 < nsub:
            gens.append(epi(step - 3))
        _round_robin(gens)
    carry_ref[...] = last_rows[nsub - 1]


def _rwkv_call(x, g, w, b, mu, w0, wup, a0, aup, gup, k_k, k_a, r_k, lnx_g, lnx_b, *, ts, sub=256):
    sub = min(sub, ts)
    B, S, D = x.shape
    width = w0.shape[-1]
    cols = w.shape[-1]
    C, GW = CHUNK, GROUP * HEAD
    bdf = jnp.kron(jnp.eye(GROUP, dtype=F32), jnp.ones((HEAD, HEAD), F32))
    ti = jnp.arange(C)[:, None]
    si = jnp.arange(C)[None, :]
    levels = []
    m = 2
    while m < C:
        lm = ((ti // m) % 2 == 1) & ((si // m) % 2 == 0) & (ti // (2 * m) == si // (2 * m))
        lm = lm.astype(F32)
        levels.append(jnp.stack([jnp.concatenate([lm, 0 * lm], axis=1), jnp.concatenate([0 * lm, lm], axis=1)]))
        m *= 2
    lvl = jnp.stack(levels).astype(BF16)
    ones = jnp.ones((C, HEAD), F32)
    halfm = jnp.stack([jnp.concatenate([ones, 0 * ones], axis=1),
                       jnp.concatenate([0 * ones, ones], axis=1)]).astype(BF16)
    eye_pk = jnp.tile(jnp.eye(C, dtype=F32), (1, GROUP))
    base_pk = jnp.tile(((ti // 2 == si // 2) & (ti > si)).astype(F32), (1, GROUP))
    tri = jnp.kron(jnp.eye(sub // C, dtype=F32), jnp.tril(jnp.ones((C, C), F32))).astype(BF16)
    row = lambda t: t.reshape(1, -1).astype(F32)
    consts = [row(g), w.astype(BF16), row(b), row(mu), row(w0), wup, row(a0), aup, gup,
              row(k_k), row(k_a), row(r_k), row(lnx_g), row(lnx_b),
              bdf.astype(BF16), bdf, lvl, halfm, eye_pk, base_pk, tri, jnp.eye(GW, dtype=F32)]
    body = functools.partial(_rwkv_body, ts=ts, sub=sub, width=width)
    return pl.pallas_call(
        body,
        grid=(B, S // ts),
        in_specs=[pl.BlockSpec((None, ts, D), lambda i, j: (i, j, 0))] + [_const_spec(c.shape) for c in consts],
        out_specs=pl.BlockSpec((None, ts, width), lambda i, j: (i, j, 0)),
        out_shape=jax.ShapeDtypeStruct((B, S, width), BF16),
        scratch_shapes=[pltpu.VMEM((SUBLANES, cols), F32), pltpu.VMEM((width // GW, GW, GW), F32)],
        compiler_params=pltpu.CompilerParams(dimension_semantics=("arbitrary", "arbitrary"),
                                             vmem_limit_bytes=VMEM_LIMIT),
        name="rwkv",
    )(x, *consts)


def _mix_body(x_ref, ya_ref, g_ref, wr_ref, br_ref, cw_ref, cb_ref, wgr_ref, bgr_ref, lam_ref,
              wg_ref, bg_ref, wa_ref, wb_ref, wm_ref, o_ref, xs_ref, hc_ref,
              *, ts, sub, width, d, kconv):
    s = pl.program_id(1)
    nsub = ts // sub

    @pl.when(s == 0)
    def _():
        xs_ref[...] = jnp.zeros_like(xs_ref)
        hc_ref[...] = jnp.zeros_like(hc_ref)

    row = lax.broadcasted_iota(jnp.int32, (sub, 1), 0)
    sub8 = row % SUBLANES
    lam = lam_ref[...]
    log_sig = jnp.minimum(lam, 0.0) - jnp.log(1.0 + jnp.exp(-jnp.abs(lam)))
    hs, ybs, heads_out, carries, tails = {}, {}, {}, {}, {}

    def rg(u):
        x = x_ref[u * sub:(u + 1) * sub, :]
        h = _rms(x, g_ref[...]).astype(BF16)
        hs[u] = h
        yield
        pxy = jnp.dot(h, wr_ref[...], preferred_element_type=F32) + br_ref[...]
        px = pxy[:, :width]
        py = pxy[:, width:]
        yield
        prev8 = xs_ref[...] if u == 0 else tails[u - 1]
        tails[u] = px[sub - SUBLANES:sub, :]
        xb = cb_ref[...] + px * cw_ref[kconv - 1:kconv, :]
        for j in range(1, kconv):
            xb = xb + _shift_rows(px, prev8, j, sub8) * cw_ref[kconv - 1 - j:kconv - j, :]
        yield
        xb_bf = xb.astype(BF16)
        half = width // 2
        g0 = jnp.dot(xb_bf[:, :half], wgr_ref[0], preferred_element_type=F32)
        g1 = jnp.dot(xb_bf[:, half:], wgr_ref[1], preferred_element_type=F32)
        ga = _sigmoid(jnp.concatenate([g0[:, :half], g1[:, :half]], axis=1) + bgr_ref[:, :width])
        gx = _sigmoid(jnp.concatenate([g0[:, half:], g1[:, half:]], axis=1) + bgr_ref[:, width:])
        yield
        log_a = LRU_C * ga * log_sig
        a = jnp.exp(log_a)
        mult = _sqrt(1.0 - jnp.exp(2.0 * log_a))
        if u == 0:
            mult = jnp.where((row == 0) & (s == 0), 1.0, mult)
        uu = xb * gx * mult
        yield
        dd = 1
        while dd < SUBLANES:
            keep = sub8 >= dd
            u_sh = _group_roll(uu, dd)
            a_sh = _group_roll(a, dd)
            uu = uu + jnp.where(keep, a * u_sh, 0.0)
            a = jnp.where(keep, a * a_sh, a)
            dd *= 2
            yield
        h_prev = hc_ref[...] if u == 0 else carries[u - 1]
        gelu_y = _gelu(py)
        yield
        outs = []
        for r0 in range(0, sub, SUBLANES):
            hg = uu[r0:r0 + SUBLANES, :] + a[r0:r0 + SUBLANES, :] * h_prev
            h_prev = hg[SUBLANES - 1:SUBLANES, :]
            outs.append(hg)
        carries[u] = h_prev
        ybs[u] = (jnp.concatenate(outs, axis=0) * gelu_y).astype(BF16)

    def head(u):
        yield
        h = hs[u]
        half = d // 2
        parts = []
        for c0 in range(0, 2 * d, half):
            parts.append(jnp.dot(h, wg_ref[:, c0:c0 + half], preferred_element_type=F32) + bg_ref[:, c0:c0 + half])
            yield
        ya = jnp.dot(ya_ref[u * sub:(u + 1) * sub, :], wa_ref[...], preferred_element_type=F32)
        heads_out[u] = (jnp.concatenate(parts[:2], axis=1), jnp.concatenate(parts[2:], axis=1), ya)

    def tail(u):
        ga, gb, ya = heads_out[u]
        yb = jnp.dot(ybs[u], wb_ref[...], preferred_element_type=F32)
        yield
        merged = (_sigmoid(ga) * ya + _sigmoid(gb) * yb).astype(BF16)
        yield
        o_ref[u * sub:(u + 1) * sub, :] = (x_ref[u * sub:(u + 1) * sub, :]
                                           + jnp.dot(merged, wm_ref[...], preferred_element_type=F32))

    for step in range(nsub + 1):
        gens = []
        if step < nsub:
            gens += [rg(step), head(step)]
        if step >= 1:
            gens.append(tail(step - 1))
        _round_robin(gens)
    hc_ref[...] = carries[nsub - 1]
    xs_ref[...] = tails[nsub - 1]


def _mix_call(x, ya, g, wr, br, conv_w, conv_b, w_ga, b_ga, w_gx, b_gx, lam, wg, bg, wa, wb, wm, *, ts, sub=256):
    sub = min(sub, ts)
    B, S, D = x.shape
    width = lam.shape[-1]
    nb = w_ga.shape[0]
    bdiag = lambda t, lo, hi: jax.scipy.linalg.block_diag(*[t[i] for i in range(lo, hi)])
    wgr = jnp.stack([jnp.concatenate([bdiag(w_ga, lo, lo + nb // 2), bdiag(w_gx, lo, lo + nb // 2)], axis=1)
                     for lo in (0, nb // 2)]).astype(BF16)
    row = lambda t: t.reshape(1, -1).astype(F32)
    consts = [row(g), wr.astype(BF16), row(br), conv_w.astype(F32), row(conv_b), wgr,
              jnp.concatenate([row(b_ga), row(b_gx)], axis=1), row(lam),
              wg.astype(BF16), row(bg), wa.astype(BF16), wb.astype(BF16), wm.astype(BF16)]
    body = functools.partial(_mix_body, ts=ts, sub=sub, width=width, d=D, kconv=conv_w.shape[0])
    tok = lambda n: pl.BlockSpec((None, ts, n), lambda i, j: (i, j, 0))
    return pl.pallas_call(
        body,
        grid=(B, S // ts),
        in_specs=[tok(D), tok(ya.shape[-1])] + [_const_spec(c.shape) for c in consts],
        out_specs=tok(D),
        out_shape=jax.ShapeDtypeStruct((B, S, D), F32),
        scratch_shapes=[pltpu.VMEM((SUBLANES, width), F32), pltpu.VMEM((1, width), F32)],
        compiler_params=pltpu.CompilerParams(dimension_semantics=("arbitrary", "arbitrary"),
                                             vmem_limit_bytes=VMEM_LIMIT),
        name="mix",
    )(x, ya, *consts)


def _kv_body(m_ref, g_ref, w_ref, o_ref):
    h = _rms(m_ref[...], g_ref[...]).astype(BF16)
    o_ref[...] = jnp.dot(h, w_ref[...], preferred_element_type=F32).astype(o_ref.dtype)


def _kv_call(mem, g, w):
    B, M, D = mem.shape
    consts = [g.reshape(1, -1).astype(F32), w.astype(BF16)]
    return pl.pallas_call(
        _kv_body,
        grid=(B,),
        in_specs=[pl.BlockSpec((None, M, D), lambda i: (i, 0, 0))] + [_const_spec(c.shape) for c in consts],
        out_specs=pl.BlockSpec((None, M, 2 * D), lambda i: (i, 0, 0)),
        out_shape=jax.ShapeDtypeStruct((B, M, 2 * D), BF16),
        compiler_params=pltpu.CompilerParams(dimension_semantics=("arbitrary",),
                                             vmem_limit_bytes=VMEM_LIMIT),
        name="memkv",
    )(mem, *consts)


def _attn_body(x_ref, kv_ref, g_ref, wq_ref, wo_ref, o_ref, *, d, heads):
    x = x_ref[...]
    h = _rms(x, g_ref[...]).astype(BF16)
    q = jnp.dot(h, wq_ref[...], preferred_element_type=F32)
    hd = d // heads
    scale = hd ** -0.5
    outs = []
    for i in range(heads):
        qh = q[:, i * hd:(i + 1) * hd].astype(BF16)
        kh = kv_ref[:, i * hd:(i + 1) * hd]
        vh = kv_ref[:, d + i * hd:d + (i + 1) * hd]
        sc = lax.dot_general(qh, kh, (((1,), (1,)), ((), ())), preferred_element_type=F32) * scale
        sc = sc - jnp.max(sc, axis=-1, keepdims=True)
        e = jnp.exp(sc)
        pr = e / jnp.sum(e, axis=-1, keepdims=True)
        outs.append(jnp.dot(pr.astype(BF16), vh, preferred_element_type=F32).astype(BF16))
    o = jnp.concatenate(outs, axis=1)
    o_ref[...] = x + jnp.dot(o, wo_ref[...], preferred_element_type=F32)


def _attn_call(x, kv, g, wq, wo, *, tm, heads):
    B, S, D = x.shape
    M = kv.shape[1]
    consts = [g.reshape(1, -1).astype(F32), wq.astype(BF16), wo.astype(BF16)]
    tok = pl.BlockSpec((None, tm, D), lambda i, j: (i, j, 0))
    return pl.pallas_call(
        functools.partial(_attn_body, d=D, heads=heads),
        grid=(B, S // tm),
        in_specs=[tok, pl.BlockSpec((None, M, 2 * D), lambda i, j: (i, 0, 0))]
        + [_const_spec(c.shape) for c in consts],
        out_specs=tok,
        out_shape=jax.ShapeDtypeStruct((B, S, D), F32),
        compiler_params=pltpu.CompilerParams(dimension_semantics=("arbitrary", "arbitrary"),
                                             vmem_limit_bytes=VMEM_LIMIT),
        name="attn",
    )(x, kv, *consts)


def _ffn_body(x_ref, g_ref, wi_ref, cw_ref, cb_ref, wo_ref, gf_ref, o_ref, us_ref, act_ref,
              *, tm, dff, ck, kconv, final_norm):
    s = pl.program_id(1)

    @pl.when(s == 0)
    def _():
        us_ref[0:SUBLANES, :] = jnp.zeros((SUBLANES, dff), F32)

    x = x_ref[...]
    h = _rms(x, g_ref[...]).astype(BF16)
    def proj_in(c0):
        return (jnp.dot(h, wi_ref[:, c0:c0 + ck], preferred_element_type=F32),
                jnp.dot(h, wi_ref[:, dff + c0:dff + c0 + ck], preferred_element_type=F32))

    starts = list(range(0, dff, ck))
    nxt = proj_in(starts[0])
    for i, c0 in enumerate(starts):
        ug, uv = nxt
        if i + 1 < len(starts):
            nxt = proj_in(starts[i + 1])
        us_ref[SUBLANES:SUBLANES + tm, c0:c0 + ck] = ug
        gate = cb_ref[:, c0:c0 + ck]
        for j in range(kconv):
            gate = gate + (us_ref[pl.ds(SUBLANES - (kconv - 1) + j, tm), c0:c0 + ck]
                           * cw_ref[j:j + 1, c0:c0 + ck])
        us_ref[0:SUBLANES, c0:c0 + ck] = ug[tm - SUBLANES:tm, :]
        act_ref[:, c0:c0 + ck] = (_gelu(gate) * uv).astype(BF16)
    acc = x + jnp.dot(act_ref[...], wo_ref[...], preferred_element_type=F32)
    if final_norm:
        acc = _rms(acc, gf_ref[...])
    o_ref[...] = acc


def _ffn_call(x, g, wi, conv_w, conv_b, wo, gf, *, tm, ck, final_norm):
    B, S, D = x.shape
    dff = wo.shape[0]
    consts = [g.reshape(1, -1).astype(F32), wi.astype(BF16), conv_w.astype(F32),
              conv_b.reshape(1, -1).astype(F32), wo.astype(BF16), gf.reshape(1, -1).astype(F32)]
    tok = pl.BlockSpec((None, tm, D), lambda i, j: (i, j, 0))
    body = functools.partial(_ffn_body, tm=tm, dff=dff, ck=ck, kconv=conv_w.shape[0], final_norm=final_norm)
    return pl.pallas_call(
        body,
        grid=(B, S // tm),
        in_specs=[tok] + [_const_spec(c.shape) for c in consts],
        out_specs=tok,
        out_shape=jax.ShapeDtypeStruct((B, S, D), F32),
        scratch_shapes=[pltpu.VMEM((tm + SUBLANES, dff), F32), pltpu.VMEM((tm, dff), BF16)],
        compiler_params=pltpu.CompilerParams(dimension_semantics=("arbitrary", "arbitrary"),
                                             vmem_limit_bytes=VMEM_LIMIT),
        name="ffn",
    )(x, *consts)


def _pick(n, pref):
    t = min(n, pref)
    while n % t:
        t //= 2
    return t


def kernel(x, mem, norm_mix_g, w_in, b_in, mu_shift, w0, w_lora_up, a0, a_lora_up, g_lora_up, k_k, k_a, r_k, lnx_g, lnx_b, w_branch_a, conv_b_w, conv_b_b, w_rg_a, b_rg_a, w_rg_x, b_rg_x, lru_lambda, w_branch_b, w_mix_out, norm_x_g, norm_mem_g, w_cq, w_ckv, w_co, norm_ffn_g, w_ffn_in, ffn_conv_w, ffn_conv_b, w_ffn_out, norm_final_g):
    depth = w_in.shape[0]
    B, S, D = x.shape
    a_width = w0.shape[-1]
    b_width = lru_lambda.shape[-1]
    n_dec, n_aaa = w_lora_up.shape[1], a_lora_up.shape[1]
    assert n_dec + n_aaa == 128, "decay/a lora widths must fill one 128-lane slab"
    cb0 = mu_shift.shape[-1]
    cb2 = cb0 + 2 * b_width
    heads = 4
    for l in range(depth):
        wup = jnp.concatenate([w_lora_up[l], jnp.zeros((n_aaa, a_width), F32)], axis=0).astype(BF16)
        aup = jnp.concatenate([jnp.zeros((n_dec, a_width), F32), a_lora_up[l]], axis=0).astype(BF16)
        ya = _rwkv_call(x, norm_mix_g[l], w_in[l][:, :cb0], b_in[l][:cb0], mu_shift[l], w0[l], wup, a0[l], aup,
                        g_lora_up[l].astype(BF16), k_k[l], k_a[l], r_k[l], lnx_g[l], lnx_b[l], ts=_pick(S, 512))
        x = _mix_call(x, ya, norm_mix_g[l], w_in[l][:, cb0:cb2], b_in[l][cb0:cb2], conv_b_w[l], conv_b_b[l],
                      w_rg_a[l], b_rg_a[l], w_rg_x[l], b_rg_x[l], lru_lambda[l], w_in[l][:, cb2:], b_in[l][cb2:],
                      w_branch_a[l], w_branch_b[l], w_mix_out[l], ts=_pick(S, 512), sub=512)
        kv = _kv_call(mem, norm_mem_g[l], w_ckv[l])
        x = _attn_call(x, kv, norm_x_g[l], w_cq[l], w_co[l], tm=_pick(S, 1024), heads=heads)
        last = l == depth - 1
        x = _ffn_call(x, norm_ffn_g[l], w_ffn_in[l], ffn_conv_w[l], ffn_conv_b[l], w_ffn_out[l], norm_final_g,
                      tm=_pick(S, 512), ck=256, final_norm=last)
    return x
```

```python
import functools
import math

import jax
import jax.numpy as jnp
from jax import lax
from jax.experimental import pallas as pl
from jax.experimental.pallas import tpu as pltpu

F32 = jnp.float32
BF16 = jnp.bfloat16

NORM_EPS = 1e-6
LNX_EPS = 64e-5
LRU_C = 8.0
HEAD = 64
CHUNK = 64
GROUP = 4
SUBLANES = 8
VMEM_LIMIT = 56 * 1024 * 1024


def _rms(x, g):
    return x * lax.rsqrt(jnp.mean(x * x, axis=-1, keepdims=True) + NORM_EPS) * g


def _sigmoid(x):
    return 1.0 / (1.0 + jnp.exp(-x))


def _gelu(x):
    c = math.sqrt(2.0 / math.pi)
    return 0.5 * x * (1.0 + jnp.tanh(c * (x + 0.044715 * (x * x * x))))


def _sqrt(x):
    return jnp.where(x > 0.0, x * lax.rsqrt(x), 0.0)


def _group_roll(z, j):
    n, lanes = z.shape
    return pltpu.roll(z.reshape(n // SUBLANES, SUBLANES, lanes), j, 1).reshape(n, lanes)


def _shift_rows(x, prev8, j, sub8):
    r = _group_roll(jnp.concatenate([prev8, x], axis=0), j)
    return jnp.where(sub8 >= j, r[SUBLANES:], r[:-SUBLANES])


def _bdot(a, b):
    return jnp.dot(a.astype(BF16), b.astype(BF16), preferred_element_type=F32)


def _split2_dot_left(m, x):
    hi = x.astype(BF16)
    lo = (x - hi.astype(F32)).astype(BF16)
    return (jnp.dot(m, hi, preferred_element_type=F32)
            + jnp.dot(m, lo, preferred_element_type=F32))


def _const_spec(shape):
    n = len(shape)
    return pl.BlockSpec(shape, lambda *_: (0,) * n)


def _round_robin(gens):
    active = list(gens)
    while active:
        alive = []
        for gen in active:
            try:
                next(gen)
                alive.append(gen)
            except StopIteration:
                pass
        active = alive


def _rwkv_body(x_ref, g_ref, w_ref, b_ref, mu_ref, w0_ref, wup_ref, a0_ref, aup_ref, gup_ref,
               kk_ref, ka_ref, rk_ref, lg_ref, lb_ref, bdm_ref, bdf_ref, lvl_ref, halfm_ref, eye_ref, base_ref,
               tri_ref, eyeg_ref, o_ref, carry_ref, p_ref, *, ts, sub, width):
    s = pl.program_id(1)
    c1, c2, c3 = width, 2 * width, 3 * width
    C = CHUNK
    GW = GROUP * HEAD
    groups = range(width // GW)
    nchunks = sub // C
    nsub = ts // sub

    @pl.when(s == 0)
    def _():
        carry_ref[...] = jnp.zeros_like(carry_ref)
        p_ref[...] = jnp.zeros_like(p_ref)

    bdm = bdm_ref[...]
    bdf = bdf_ref[...]
    eye_pk = eye_ref[...]
    base_pk = base_ref[...]
    eye_gw = eyeg_ref[...]
    nlev = lvl_ref.shape[0]
    sub8 = lax.broadcasted_iota(jnp.int32, (sub, 1), 0) % SUBLANES
    ti = lax.broadcasted_iota(jnp.int32, (C, GW), 0)
    si = lax.broadcasted_iota(jnp.int32, (C, GW), 1) % HEAD
    strict = ti > si
    incl = ti >= si

    def head_sums(t):
        return jnp.concatenate([jnp.dot(t[:, i * GW:(i + 1) * GW].astype(BF16), bdm, preferred_element_type=F32)
                                for i in groups], axis=1)

    def bd_build(t, m):
        zero = jnp.zeros((C, 128), BF16)
        blocks = []
        for hd in range(GROUP):
            col = hd * HEAD // 128
            cols = [zero] * (GW // 128)
            cols[col] = t[:, col * 128:(col + 1) * 128] * m[hd * HEAD % 128 // HEAD]
            blocks.append(jnp.concatenate(cols, axis=1))
        return jnp.concatenate(blocks, axis=0)

    halfm = (halfm_ref[0], halfm_ref[1])

    def bd_of(t):
        return bd_build(t, halfm)

    pro_out = {}
    last_rows = {}
    res = {}
    ys = {}
    state = {}

    def pro(u):
        h = _rms(x_ref[u * sub:(u + 1) * sub, :], g_ref[...]).astype(BF16)
        p = jnp.dot(h, w_ref[...], preferred_element_type=F32) + b_ref[...]
        yield
        first = carry_ref[...] if u == 0 else last_rows[u - 1]
        last_rows[u] = p[sub - SUBLANES:sub, :]
        prev = _shift_rows(p, first, 1, sub8)
        p = p + (prev - p) * mu_ref[...]
        r = p[:, :c1]
        k = p[:, c1:c2]
        v = p[:, c2:c3]
        lora = p[:, c3:c3 + 128]
        pg = p[:, c3 + 128:]
        yield
        w_pre = w0_ref[...] + _bdot(jnp.tanh(lora), wup_ref[...])
        z = -w_pre
        w_log = -(jnp.maximum(z, 0.0) + jnp.log(1.0 + jnp.exp(-jnp.abs(z)))) - 0.5
        ew = jnp.exp(w_log)
        yield
        a = _sigmoid(a0_ref[...] + _bdot(lora, aup_ref[...]))
        g = _bdot(_sigmoid(pg), gup_ref[...])
        yield
        kk = k * kk_ref[...]
        k2 = k * (1.0 + (a - 1.0) * ka_ref[...])
        kk = kk * jnp.minimum(lax.rsqrt(head_sums(kk * kk)), 1e12)
        yield
        bonus = head_sums(r * k2 * rk_ref[...]) * v
        yield
        cl = _split2_dot_left(tri_ref[...], ew)
        yield
        w_inv = jnp.exp(cl)
        at = -kk * jnp.exp(ew - cl)
        yield
        rt = r * jnp.exp(-cl)
        bt = (kk * a) * w_inv
        kt = k2 * w_inv
        pro_out[u] = (at, rt, bt, kt, v, cl, bonus, g)

    def indep(u, c, gi):
        at, rt, bt, kt, v, cl = pro_out[u][:6]
        rs = slice(c * C, (c + 1) * C)
        ls = slice(gi * GW, (gi + 1) * GW)
        a_c, r_c, b_c, k_c, v_c = at[rs, ls], rt[rs, ls], bt[rs, ls], kt[rs, ls], v[rs, ls]
        lhs = jnp.concatenate([a_c, r_c], axis=0).astype(BF16)
        rhs = jnp.concatenate([bd_of(b_c.astype(BF16)), bd_of(k_c.astype(BF16))], axis=0)
        A = lax.dot_general(lhs, rhs, (((1,), (1,)), ((), ())), preferred_element_type=F32)
        yield
        n = jnp.where(strict, A[:C, :GW], 0.0)
        aak = jnp.where(strict, A[:C, GW:], 0.0).astype(BF16)
        arbk = jnp.concatenate([jnp.where(incl, A[C:, :GW], 0.0), jnp.where(incl, A[C:, GW:], 0.0)],
                               axis=1).astype(BF16)
        n_bf = n.astype(BF16)
        T = eye_pk + n * base_pk
        for lv in range(nlev):
            TM = jnp.dot(T.astype(BF16), bd_build(n_bf, (lvl_ref[lv, 0], lvl_ref[lv, 1])),
                         preferred_element_type=F32)
            yield
            T = T + jnp.dot(TM.astype(BF16), bd_of(T.astype(BF16)), preferred_element_type=F32)
            yield
        v_bf = v_c.astype(BF16)
        bdv = bd_of(v_bf)
        AV = jnp.dot(aak, bdv, preferred_element_type=F32)
        yield
        wc = jnp.exp(-cl[(c + 1) * C - 1:(c + 1) * C, ls])
        hat = jnp.concatenate([b_c * wc, k_c * wc], axis=0).astype(BF16)
        wcol = jnp.sum(eye_gw * wc, axis=1, keepdims=True)
        res[(u, c, gi)] = (lhs, T.astype(BF16), AV, arbk, bdv, hat, v_bf, wcol)

    def dep(u, gi):
        P = p_ref[gi] if u == 0 else state[gi]
        for c in range(nchunks):
            lhs, T_bf, AV, arbk, bdv, hat, v_bf, wcol = res[(u, c, gi)]
            AP = jnp.dot(lhs, P.astype(BF16), preferred_element_type=F32)
            yield
            U = jnp.dot(T_bf, bd_of((AP[:C] + AV).astype(BF16)), preferred_element_type=F32)
            yield
            U_bf = U.astype(BF16)
            ys[(u, c, gi)] = AP[C:] + jnp.dot(arbk, jnp.concatenate([bd_of(U_bf), bdv], axis=0),
                                              preferred_element_type=F32)
            upd = lax.dot_general(hat, jnp.concatenate([U_bf, v_bf], axis=0), (((0,), (0,)), ((), ())),
                                  preferred_element_type=F32)
            yield
            P = P * wcol + upd * bdf
        state[gi] = P
        if u == nsub - 1:
            p_ref[gi] = P

    def epi(u):
        bonus, g = pro_out[u][6:]
        y = jnp.concatenate([jnp.concatenate([ys[(u, c, gi)] for gi in groups], axis=1)
                             for c in range(nchunks)], axis=0)
        inv_n = 1.0 / HEAD
        mean = head_sums(y) * inv_n
        yield
        yc = y - mean
        var = head_sums(yc * yc) * inv_n
        yield
        yn = yc * lax.rsqrt(var + LNX_EPS) * lg_ref[...] + lb_ref[...]
        o_ref[u * sub:(u + 1) * sub, :] = ((yn + bonus) * g).astype(o_ref.dtype)

    for step in range(nsub + 3):
        gens = []
        if step < nsub:
            gens.append(pro(step))
        if 0 <= step - 1 < nsub:
            gens += [indep(step - 1, c, gi) for c in range(nchunks) for gi in groups]
        if 0 <= step - 2 < nsub:
            gens += [dep(step - 2, gi) for gi in groups]
        if 0 <= step - 3 < nsub:
            gens.append(epi(step - 3))
        _round_robin(gens)
    carry_ref[...] = last_rows[nsub - 1]


def _rwkv_call(x, g, w, b, mu, w0, wup, a0, aup, gup, k_k, k_a, r_k, lnx_g, lnx_b, *, ts, sub=256):
    sub = min(sub, ts)
    B, S, D = x.shape
    width = w0.shape[-1]
    cols = w.shape[-1]
    C, GW = CHUNK, GROUP * HEAD
    bdf = jnp.kron(jnp.eye(GROUP, dtype=F32), jnp.ones((HEAD, HEAD), F32))
    ti = jnp.arange(C)[:, None]
    si = jnp.arange(C)[None, :]
    levels = []
    m = 2
    while m < C:
        lm = ((ti // m) % 2 == 1) & ((si // m) % 2 == 0) & (ti // (2 * m) == si // (2 * m))
        lm = lm.astype(F32)
        levels.append(jnp.stack([jnp.concatenate([lm, 0 * lm], axis=1), jnp.concatenate([0 * lm, lm], axis=1)]))
        m *= 2
    lvl = jnp.stack(levels).astype(BF16)
    ones = jnp.ones((C, HEAD), F32)
    halfm = jnp.stack([jnp.concatenate([ones, 0 * ones], axis=1),
                       jnp.concatenate([0 * ones, ones], axis=1)]).astype(BF16)
    eye_pk = jnp.tile(jnp.eye(C, dtype=F32), (1, GROUP))
    base_pk = jnp.tile(((ti // 2 == si // 2) & (ti > si)).astype(F32), (1, GROUP))
    tri = jnp.kron(jnp.eye(sub // C, dtype=F32), jnp.tril(jnp.ones((C, C), F32))).astype(BF16)
    row = lambda t: t.reshape(1, -1).astype(F32)
    consts = [row(g), w.astype(BF16), row(b), row(mu), row(w0), wup, row(a0), aup, gup,
              row(k_k), row(k_a), row(r_k), row(lnx_g), row(lnx_b),
              bdf.astype(BF16), bdf, lvl, halfm, eye_pk, base_pk, tri, jnp.eye(GW, dtype=F32)]
    body = functools.partial(_rwkv_body, ts=ts, sub=sub, width=width)
    return pl.pallas_call(
        body,
        grid=(B, S // ts),
        in_specs=[pl.BlockSpec((None, ts, D), lambda i, j: (i, j, 0))] + [_const_spec(c.shape) for c in consts],
        out_specs=pl.BlockSpec((None, ts, width), lambda i, j: (i, j, 0)),
        out_shape=jax.ShapeDtypeStruct((B, S, width), BF16),
        scratch_shapes=[pltpu.VMEM((SUBLANES, cols), F32), pltpu.VMEM((width // GW, GW, GW), F32)],
        compiler_params=pltpu.CompilerParams(dimension_semantics=("arbitrary", "arbitrary"),
                                             vmem_limit_bytes=VMEM_LIMIT),
        name="rwkv",
    )(x, *consts)


def _mix_body(x_ref, ya_ref, g_ref, wr_ref, br_ref, cw_ref, cb_ref, wgr_ref, bgr_ref, lam_ref,
              wg_ref, bg_ref, wa_ref, wb_ref, wm_ref, o_ref, xs_ref, hc_ref,
              *, ts, sub, width, d, kconv):
    s = pl.program_id(1)
    nsub = ts // sub

    @pl.when(s == 0)
    def _():
        xs_ref[...] = jnp.zeros_like(xs_ref)
        hc_ref[...] = jnp.zeros_like(hc_ref)

    row = lax.broadcasted_iota(jnp.int32, (sub, 1), 0)
    sub8 = row % SUBLANES
    lam = lam_ref[...]
    log_sig = jnp.minimum(lam, 0.0) - jnp.log(1.0 + jnp.exp(-jnp.abs(lam)))
    hs, ybs, heads_out, carries, tails = {}, {}, {}, {}, {}

    def rg(u):
        x = x_ref[u * sub:(u + 1) * sub, :]
        h = _rms(x, g_ref[...]).astype(BF16)
        hs[u] = h
        yield
        pxy = jnp.dot(h, wr_ref[...], preferred_element_type=F32) + br_ref[...]
        px = pxy[:, :width]
        py = pxy[:, width:]
        yield
        prev8 = xs_ref[...] if u == 0 else tails[u - 1]
        tails[u] = px[sub - SUBLANES:sub, :]
        xb = cb_ref[...] + px * cw_ref[kconv - 1:kconv, :]
        for j in range(1, kconv):
            xb = xb + _shift_rows(px, prev8, j, sub8) * cw_ref[kconv - 1 - j:kconv - j, :]
        yield
        xb_bf = xb.astype(BF16)
        half = width // 2
        g0 = jnp.dot(xb_bf[:, :half], wgr_ref[0], preferred_element_type=F32)
        g1 = jnp.dot(xb_bf[:, half:], wgr_ref[1], preferred_element_type=F32)
        ga = _sigmoid(jnp.concatenate([g0[:, :half], g1[:, :half]], axis=1) + bgr_ref[:, :width])
        gx = _sigmoid(jnp.concatenate([g0[:, half:], g1[:, half:]], axis=1) + bgr_ref[:, width:])
        yield
        log_a = LRU_C * ga * log_sig
        a = jnp.exp(log_a)
        mult = _sqrt(1.0 - jnp.exp(2.0 * log_a))
        if u == 0:
            mult = jnp.where((row == 0) & (s == 0), 1.0, mult)
        uu = xb * gx * mult
        yield
        dd = 1
        while dd < SUBLANES:
            keep = sub8 >= dd
            u_sh = _group_roll(uu, dd)
            a_sh = _group_roll(a, dd)
            uu = uu + jnp.where(keep, a * u_sh, 0.0)
            a = jnp.where(keep, a * a_sh, a)
            dd *= 2
            yield
        h_prev = hc_ref[...] if u == 0 else carries[u - 1]
        gelu_y = _gelu(py)
        yield
        outs = []
        for r0 in range(0, sub, SUBLANES):
            hg = uu[r0:r0 + SUBLANES, :] + a[r0:r0 + SUBLANES, :] * h_prev
            h_prev = hg[SUBLANES - 1:SUBLANES, :]
            outs.append(hg)
        carries[u] = h_prev
        ybs[u] = (jnp.concatenate(outs, axis=0) * gelu_y).astype(BF16)

    def head(u):
        yield
        h = hs[u]
        half = d // 2
        parts = []
        for c0 in range(0, 2 * d, half):
            parts.append(jnp.dot(h, wg_ref[:, c0:c0 + half], preferred_element_type=F32) + bg_ref[:, c0:c0 + half])
            yield
        ya = jnp.dot(ya_ref[u * sub:(u + 1) * sub, :], wa_ref[...], preferred_element_type=F32)
        heads_out[u] = (jnp.concatenate(parts[:2], axis=1), jnp.concatenate(parts[2:], axis=1), ya)

    def tail(u):
        ga, gb, ya = heads_out[u]
        yb = jnp.dot(ybs[u], wb_ref[...], preferred_element_type=F32)
        yield
        merged = (_sigmoid(ga) * ya + _sigmoid(gb) * yb).astype(BF16)
        yield
        o_ref[u * sub:(u + 1) * sub, :] = (x_ref[u * sub:(u + 1) * sub, :]
                                           + jnp.dot(merged, wm_ref[...], preferred_element_type=F32))

    for step in range(nsub + 1):
        gens = []
        if step < nsub:
            gens += [rg(step), head(step)]
        if step >= 1:
            gens.append(tail(step - 1))
        _round_robin(gens)
    hc_ref[...] = carries[nsub - 1]
    xs_ref[...] = tails[nsub - 1]


def _mix_call(x, ya, g, wr, br, conv_w, conv_b, w_ga, b_ga, w_gx, b_gx, lam, wg, bg, wa, wb, wm, *, ts, sub=256):
    sub = min(sub, ts)
    B, S, D = x.shape
    width = lam.shape[-1]
    nb = w_ga.shape[0]
    bdiag = lambda t, lo, hi: jax.scipy.linalg.block_diag(*[t[i] for i in range(lo, hi)])
    wgr = jnp.stack([jnp.concatenate([bdiag(w_ga, lo, lo + nb // 2), bdiag(w_gx, lo, lo + nb // 2)], axis=1)
                     for lo in (0, nb // 2)]).astype(BF16)
    row = lambda t: t.reshape(1, -1).astype(F32)
    consts = [row(g), wr.astype(BF16), row(br), conv_w.astype(F32), row(conv_b), wgr,
              jnp.concatenate([row(b_ga), row(b_gx)], axis=1), row(lam),
              wg.astype(BF16), row(bg), wa.astype(BF16), wb.astype(BF16), wm.astype(BF16)]
    body = functools.partial(_mix_body, ts=ts, sub=sub, width=width, d=D, kconv=conv_w.shape[0])
    tok = lambda n: pl.BlockSpec((None, ts, n), lambda i, j: (i, j, 0))
    return pl.pallas_call(
        body,
        grid=(B, S // ts),
        in_specs=[tok(D), tok(ya.shape[-1])] + [_const_spec(c.shape) for c in consts],
        out_specs=tok(D),
        out_shape=jax.ShapeDtypeStruct((B, S, D), F32),
        scratch_shapes=[pltpu.VMEM((SUBLANES, width), F32), pltpu.VMEM((1, width), F32)],
        compiler_params=pltpu.CompilerParams(dimension_semantics=("arbitrary", "arbitrary"),
                                             vmem_limit_bytes=VMEM_LIMIT),
        name="mix",
    )(x, ya, *consts)


def _kv_body(m_ref, g_ref, w_ref, o_ref):
    h = _rms(m_ref[...], g_ref[...]).astype(BF16)
    o_ref[...] = jnp.dot(h, w_ref[...], preferred_element_type=F32).astype(o_ref.dtype)


def _kv_call(mem, g, w):
    B, M, D = mem.shape
    consts = [g.reshape(1, -1).astype(F32), w.astype(BF16)]
    return pl.pallas_call(
        _kv_body,
        grid=(B,),
        in_specs=[pl.BlockSpec((None, M, D), lambda i: (i, 0, 0))] + [_const_spec(c.shape) for c in consts],
        out_specs=pl.BlockSpec((None, M, 2 * D), lambda i: (i, 0, 0)),
        out_shape=jax.ShapeDtypeStruct((B, M, 2 * D), BF16),
        compiler_params=pltpu.CompilerParams(dimension_semantics=("arbitrary",),
                                             vmem_limit_bytes=VMEM_LIMIT),
        name="memkv",
    )(mem, *consts)


def _attn_body(x_ref, kv_ref, g_ref, wq_ref, wo_ref, o_ref, *, d, heads):
    x = x_ref[...]
    h = _rms(x, g_ref[...]).astype(BF16)
    q = jnp.dot(h, wq_ref[...], preferred_element_type=F32)
    hd = d // heads
    scale = hd ** -0.5
    hr = range(heads)
    sc = [lax.dot_general(q[:, i * hd:(i + 1) * hd].astype(BF16), kv_ref[:, i * hd:(i + 1) * hd],
                          (((1,), (1,)), ((), ())), preferred_element_type=F32) * scale for i in hr]
    e = [jnp.exp(sc[i] - jnp.max(sc[i], axis=-1, keepdims=True)) for i in hr]
    pr = [(e[i] / jnp.sum(e[i], axis=-1, keepdims=True)).astype(BF16) for i in hr]
    outs = [jnp.dot(pr[i], kv_ref[:, d + i * hd:d + (i + 1) * hd], preferred_element_type=F32).astype(BF16)
            for i in hr]
    o = jnp.concatenate(outs, axis=1)
    o_ref[...] = x + jnp.dot(o, wo_ref[...], preferred_element_type=F32)


def _attn_call(x, kv, g, wq, wo, *, tm, heads):
    B, S, D = x.shape
    M = kv.shape[1]
    consts = [g.reshape(1, -1).astype(F32), wq.astype(BF16), wo.astype(BF16)]
    tok = pl.BlockSpec((None, tm, D), lambda i, j: (i, j, 0))
    return pl.pallas_call(
        functools.partial(_attn_body, d=D, heads=heads),
        grid=(B, S // tm),
        in_specs=[tok, pl.BlockSpec((None, M, 2 * D), lambda i, j: (i, 0, 0))]
        + [_const_spec(c.shape) for c in consts],
        out_specs=tok,
        out_shape=jax.ShapeDtypeStruct((B, S, D), F32),
        compiler_params=pltpu.CompilerParams(dimension_semantics=("arbitrary", "arbitrary"),
                                             vmem_limit_bytes=VMEM_LIMIT),
        name="attn",
    )(x, kv, *consts)


def _ffn_body(x_ref, g_ref, wi_ref, cw_ref, cb_ref, wo_ref, gf_ref, o_ref, us_ref, act_ref,
              *, tm, dff, ck, kconv, final_norm):
    s = pl.program_id(1)

    @pl.when(s == 0)
    def _():
        us_ref[0:SUBLANES, :] = jnp.zeros((SUBLANES, dff), F32)

    x = x_ref[...]
    h = _rms(x, g_ref[...]).astype(BF16)
    def proj_in(c0):
        return (jnp.dot(h, wi_ref[:, c0:c0 + ck], preferred_element_type=F32),
                jnp.dot(h, wi_ref[:, dff + c0:dff + c0 + ck], preferred_element_type=F32))

    starts = list(range(0, dff, ck))
    nxt = proj_in(starts[0])
    for i, c0 in enumerate(starts):
        ug, uv = nxt
        if i + 1 < len(starts):
            nxt = proj_in(starts[i + 1])
        us_ref[SUBLANES:SUBLANES + tm, c0:c0 + ck] = ug
        gate = cb_ref[:, c0:c0 + ck]
        for j in range(kconv):
            gate = gate + (us_ref[pl.ds(SUBLANES - (kconv - 1) + j, tm), c0:c0 + ck]
                           * cw_ref[j:j + 1, c0:c0 + ck])
        us_ref[0:SUBLANES, c0:c0 + ck] = ug[tm - SUBLANES:tm, :]
        act_ref[:, c0:c0 + ck] = (_gelu(gate) * uv).astype(BF16)
    acc = x + jnp.dot(act_ref[...], wo_ref[...], preferred_element_type=F32)
    if final_norm:
        acc = _rms(acc, gf_ref[...])
    o_ref[...] = acc


def _ffn_call(x, g, wi, conv_w, conv_b, wo, gf, *, tm, ck, final_norm):
    B, S, D = x.shape
    dff = wo.shape[0]
    consts = [g.reshape(1, -1).astype(F32), wi.astype(BF16), conv_w.astype(F32),
              conv_b.reshape(1, -1).astype(F32), wo.astype(BF16), gf.reshape(1, -1).astype(F32)]
    tok = pl.BlockSpec((None, tm, D), lambda i, j: (i, j, 0))
    body = functools.partial(_ffn_body, tm=tm, dff=dff, ck=ck, kconv=conv_w.shape[0], final_norm=final_norm)
    return pl.pallas_call(
        body,
        grid=(B, S // tm),
        in_specs=[tok] + [_const_spec(c.shape) for c in consts],
        out_specs=tok,
        out_shape=jax.ShapeDtypeStruct((B, S, D), F32),
        scratch_shapes=[pltpu.VMEM((tm + SUBLANES, dff), F32), pltpu.VMEM((tm, dff), BF16)],
        compiler_params=pltpu.CompilerParams(dimension_semantics=("arbitrary", "arbitrary"),
                                             vmem_limit_bytes=VMEM_LIMIT),
        name="ffn",
    )(x, *consts)


def _pick(n, pref):
    t = min(n, pref)
    while n % t:
        t //= 2
    return t


def kernel(x, mem, norm_mix_g, w_in, b_in, mu_shift, w0, w_lora_up, a0, a_lora_up, g_lora_up, k_k, k_a, r_k, lnx_g, lnx_b, w_branch_a, conv_b_w, conv_b_b, w_rg_a, b_rg_a, w_rg_x, b_rg_x, lru_lambda, w_branch_b, w_mix_out, norm_x_g, norm_mem_g, w_cq, w_ckv, w_co, norm_ffn_g, w_ffn_in, ffn_conv_w, ffn_conv_b, w_ffn_out, norm_final_g):
    depth = w_in.shape[0]
    B, S, D = x.shape
    a_width = w0.shape[-1]
    b_width = lru_lambda.shape[-1]
    n_dec, n_aaa = w_lora_up.shape[1], a_lora_up.shape[1]
    assert n_dec + n_aaa == 128, "decay/a lora widths must fill one 128-lane slab"
    cb0 = mu_shift.shape[-1]
    cb2 = cb0 + 2 * b_width
    heads = 4
    for l in range(depth):
        wup = jnp.concatenate([w_lora_up[l], jnp.zeros((n_aaa, a_width), F32)], axis=0).astype(BF16)
        aup = jnp.concatenate([jnp.zeros((n_dec, a_width), F32), a_lora_up[l]], axis=0).astype(BF16)
        ya = _rwkv_call(x, norm_mix_g[l], w_in[l][:, :cb0], b_in[l][:cb0], mu_shift[l], w0[l], wup, a0[l], aup,
                        g_lora_up[l].astype(BF16), k_k[l], k_a[l], r_k[l], lnx_g[l], lnx_b[l], ts=_pick(S, 512))
        x = _mix_call(x, ya, norm_mix_g[l], w_in[l][:, cb0:cb2], b_in[l][cb0:cb2], conv_b_w[l], conv_b_b[l],
                      w_rg_a[l], b_rg_a[l], w_rg_x[l], b_rg_x[l], lru_lambda[l], w_in[l][:, cb2:], b_in[l][cb2:],
                      w_branch_a[l], w_branch_b[l], w_mix_out[l], ts=_pick(S, 512), sub=512)
        kv = _kv_call(mem, norm_mem_g[l], w_ckv[l])
        x = _attn_call(x, kv, norm_x_g[l], w_cq[l], w_co[l], tm=_pick(S, 1024), heads=heads)
        last = l == depth - 1
        x = _ffn_call(x, norm_ffn_g[l], w_ffn_in[l], ffn_conv_w[l], ffn_conv_b[l], w_ffn_out[l], norm_final_g,
                      tm=_pick(S, 512), ck=256, final_norm=last)
    return x
```

```python
import functools
import math

import jax
import jax.numpy as jnp
from jax import lax
from jax.experimental import pallas as pl
from jax.experimental.pallas import tpu as pltpu

F32 = jnp.float32
BF16 = jnp.bfloat16

NORM_EPS = 1e-6
LNX_EPS = 64e-5
LRU_C = 8.0
HEAD = 64
CHUNK = 64
GROUP = 4
SUBLANES = 8
VMEM_LIMIT = 56 * 1024 * 1024


def _rms(x, g):
    return x * lax.rsqrt(jnp.mean(x * x, axis=-1, keepdims=True) + NORM_EPS) * g


def _sigmoid(x):
    return 1.0 / (1.0 + jnp.exp(-x))


def _gelu(x):
    c = math.sqrt(2.0 / math.pi)
    return 0.5 * x * (1.0 + jnp.tanh(c * (x + 0.044715 * (x * x * x))))


def _sqrt(x):
    return jnp.where(x > 0.0, x * lax.rsqrt(x), 0.0)


def _group_roll(z, j):
    n, lanes = z.shape
    return pltpu.roll(z.reshape(n // SUBLANES, SUBLANES, lanes), j, 1).reshape(n, lanes)


def _shift_rows(x, prev8, j, sub8):
    r = _group_roll(jnp.concatenate([prev8, x], axis=0), j)
    return jnp.where(sub8 >= j, r[SUBLANES:], r[:-SUBLANES])


def _bdot(a, b):
    return jnp.dot(a.astype(BF16), b.astype(BF16), preferred_element_type=F32)


def _split2_dot_left(m, x):
    hi = x.astype(BF16)
    lo = (x - hi.astype(F32)).astype(BF16)
    return (jnp.dot(m, hi, preferred_element_type=F32)
            + jnp.dot(m, lo, preferred_element_type=F32))


def _const_spec(shape):
    n = len(shape)
    return pl.BlockSpec(shape, lambda *_: (0,) * n)


def _round_robin(gens):
    active = list(gens)
    while active:
        alive = []
        for gen in active:
            try:
                next(gen)
                alive.append(gen)
            except StopIteration:
                pass
        active = alive


def _rwkv_body(x_ref, g_ref, w_ref, b_ref, mu_ref, w0_ref, wup_ref, a0_ref, aup_ref, gup_ref,
               kk_ref, ka_ref, rk_ref, lg_ref, lb_ref, bdm_ref, bdf_ref, lvl_ref, halfm_ref, eye_ref, base_ref,
               tri_ref, eyeg_ref, o_ref, carry_ref, p_ref, *, ts, sub, width):
    s = pl.program_id(1)
    c1, c2, c3 = width, 2 * width, 3 * width
    C = CHUNK
    GW = GROUP * HEAD
    groups = range(width // GW)
    nchunks = sub // C
    nsub = ts // sub

    @pl.when(s == 0)
    def _():
        carry_ref[...] = jnp.zeros_like(carry_ref)
        p_ref[...] = jnp.zeros_like(p_ref)

    bdm = bdm_ref[...]
    bdf = bdf_ref[...]
    eye_pk = eye_ref[...]
    base_pk = base_ref[...]
    eye_gw = eyeg_ref[...]
    nlev = lvl_ref.shape[0]
    sub8 = lax.broadcasted_iota(jnp.int32, (sub, 1), 0) % SUBLANES
    ti = lax.broadcasted_iota(jnp.int32, (C, GW), 0)
    si = lax.broadcasted_iota(jnp.int32, (C, GW), 1) % HEAD
    strict = ti > si
    incl = ti >= si

    def head_sums(t):
        return jnp.concatenate([jnp.dot(t[:, i * GW:(i + 1) * GW].astype(BF16), bdm, preferred_element_type=F32)
                                for i in groups], axis=1)

    def bd_build(t, m):
        zero = jnp.zeros((C, 128), BF16)
        blocks = []
        for hd in range(GROUP):
            col = hd * HEAD // 128
            cols = [zero] * (GW // 128)
            cols[col] = t[:, col * 128:(col + 1) * 128] * m[hd * HEAD % 128 // HEAD]
            blocks.append(jnp.concatenate(cols, axis=1))
        return jnp.concatenate(blocks, axis=0)

    halfm = (halfm_ref[0], halfm_ref[1])
    zero128 = jnp.zeros((C, 128), BF16)
    assert C == HEAD and 2 * HEAD == 128

    def bd_of(t):
        return bd_build(t, halfm)

    pro_out = {}
    last_rows = {}
    res = {}
    ys = {}
    state = {}

    def pro(u):
        h = _rms(x_ref[u * sub:(u + 1) * sub, :], g_ref[...]).astype(BF16)
        p = jnp.dot(h, w_ref[...], preferred_element_type=F32) + b_ref[...]
        yield
        first = carry_ref[...] if u == 0 else last_rows[u - 1]
        last_rows[u] = p[sub - SUBLANES:sub, :]
        prev = _shift_rows(p, first, 1, sub8)
        p = p + (prev - p) * mu_ref[...]
        r = p[:, :c1]
        k = p[:, c1:c2]
        v = p[:, c2:c3]
        lora = p[:, c3:c3 + 128]
        pg = p[:, c3 + 128:]
        yield
        w_pre = w0_ref[...] + _bdot(jnp.tanh(lora), wup_ref[...])
        z = -w_pre
        w_log = -(jnp.maximum(z, 0.0) + jnp.log(1.0 + jnp.exp(-jnp.abs(z)))) - 0.5
        ew = jnp.exp(w_log)
        yield
        a = _sigmoid(a0_ref[...] + _bdot(lora, aup_ref[...]))
        g = _bdot(_sigmoid(pg), gup_ref[...])
        yield
        kk = k * kk_ref[...]
        k2 = k * (1.0 + (a - 1.0) * ka_ref[...])
        kk = kk * jnp.minimum(lax.rsqrt(head_sums(kk * kk)), 1e12)
        yield
        bonus = head_sums(r * k2 * rk_ref[...]) * v
        yield
        cl = _split2_dot_left(tri_ref[...], ew)
        yield
        w_inv = jnp.exp(cl)
        at = -kk * jnp.exp(ew - cl)
        yield
        rt = r * jnp.exp(-cl)
        bt = (kk * a) * w_inv
        kt = k2 * w_inv
        pro_out[u] = (at, rt, bt, kt, v, cl, bonus, g, bt.T, kt.T)

    def indep(u, c, gi):
        at, rt, bt, kt, v, cl = pro_out[u][:6]
        btT, ktT = pro_out[u][8:]
        rs = slice(c * C, (c + 1) * C)
        ls = slice(gi * GW, (gi + 1) * GW)
        a_c, r_c, b_c, k_c, v_c = at[rs, ls], rt[rs, ls], bt[rs, ls], kt[rs, ls], v[rs, ls]
        lhs = jnp.concatenate([a_c, r_c], axis=0).astype(BF16)

        def bd_t(xt):
            blocks = []
            for hd in range(GROUP):
                r0 = gi * GW + hd * HEAD
                col = xt[r0:r0 + HEAD, (c // 2) * 128:(c // 2 + 1) * 128]
                if hd % 2 != c % 2:
                    col = pltpu.roll(col, HEAD, 1)
                cols = [zero128] * (GW // 128)
                cols[hd * HEAD // 128] = col.astype(BF16) * halfm[hd % 2]
                blocks.append(jnp.concatenate(cols, axis=1))
            return jnp.concatenate(blocks, axis=0)

        rhs = jnp.concatenate([bd_t(btT), bd_t(ktT)], axis=1)
        A = jnp.dot(lhs, rhs, preferred_element_type=F32)
        yield
        n = jnp.where(strict, A[:C, :GW], 0.0)
        aak = jnp.where(strict, A[:C, GW:], 0.0).astype(BF16)
        arbk = jnp.concatenate([jnp.where(incl, A[C:, :GW], 0.0), jnp.where(incl, A[C:, GW:], 0.0)],
                               axis=1).astype(BF16)
        n_bf = n.astype(BF16)
        T = eye_pk + n * base_pk
        for lv in range(nlev):
            TM = jnp.dot(T.astype(BF16), bd_build(n_bf, (lvl_ref[lv, 0], lvl_ref[lv, 1])),
                         preferred_element_type=F32)
            yield
            T = T + jnp.dot(TM.astype(BF16), bd_of(T.astype(BF16)), preferred_element_type=F32)
            yield
        v_bf = v_c.astype(BF16)
        bdv = bd_of(v_bf)
        AV = jnp.dot(aak, bdv, preferred_element_type=F32)
        yield
        wc = jnp.exp(-cl[(c + 1) * C - 1:(c + 1) * C, ls])
        hat = jnp.concatenate([b_c * wc, k_c * wc], axis=0).astype(BF16)
        wcol = jnp.sum(eye_gw * wc, axis=1, keepdims=True)
        res[(u, c, gi)] = (lhs, T.astype(BF16), AV, arbk, bdv, hat, v_bf, wcol)

    def dep(u, gi):
        P = p_ref[gi] if u == 0 else state[gi]
        for c in range(nchunks):
            lhs, T_bf, AV, arbk, bdv, hat, v_bf, wcol = res[(u, c, gi)]
            AP = jnp.dot(lhs, P.astype(BF16), preferred_element_type=F32)
            yield
            U = jnp.dot(T_bf, bd_of((AP[:C] + AV).astype(BF16)), preferred_element_type=F32)
            yield
            U_bf = U.astype(BF16)
            ys[(u, c, gi)] = AP[C:] + jnp.dot(arbk, jnp.concatenate([bd_of(U_bf), bdv], axis=0),
                                              preferred_element_type=F32)
            upd = lax.dot_general(hat, jnp.concatenate([U_bf, v_bf], axis=0), (((0,), (0,)), ((), ())),
                                  preferred_element_type=F32)
            yield
            P = P * wcol + upd * bdf
        state[gi] = P
        if u == nsub - 1:
            p_ref[gi] = P

    def epi(u):
        bonus, g = pro_out[u][6:8]
        y = jnp.concatenate([jnp.concatenate([ys[(u, c, gi)] for gi in groups], axis=1)
                             for c in range(nchunks)], axis=0)
        inv_n = 1.0 / HEAD
        mean = head_sums(y) * inv_n
        yield
        yc = y - mean
        var = head_sums(yc * yc) * inv_n
        yield
        yn = yc * lax.rsqrt(var + LNX_EPS) * lg_ref[...] + lb_ref[...]
        o_ref[u * sub:(u + 1) * sub, :] = ((yn + bonus) * g).astype(o_ref.dtype)

    for step in range(nsub + 3):
        gens = []
        if step < nsub:
            gens.append(pro(step))
        if 0 <= step - 1 < nsub:
            gens += [indep(step - 1, c, gi) for c in range(nchunks) for gi in groups]
        if 0 <= step - 2 < nsub:
            gens += [dep(step - 2, gi) for gi in groups]
        if 0 <= step - 3 < nsub:
            gens.append(epi(step - 3))
        _round_robin(gens)
    carry_ref[...] = last_rows[nsub - 1]


def _rwkv_call(x, g, w, b, mu, w0, wup, a0, aup, gup, k_k, k_a, r_k, lnx_g, lnx_b, *, ts, sub=256):
    sub = min(sub, ts)
    B, S, D = x.shape
    width = w0.shape[-1]
    cols = w.shape[-1]
    C, GW = CHUNK, GROUP * HEAD
    bdf = jnp.kron(jnp.eye(GROUP, dtype=F32), jnp.ones((HEAD, HEAD), F32))
    ti = jnp.arange(C)[:, None]
    si = jnp.arange(C)[None, :]
    levels = []
    m = 2
    while m < C:
        lm = ((ti // m) % 2 == 1) & ((si // m) % 2 == 0) & (ti // (2 * m) == si // (2 * m))
        lm = lm.astype(F32)
        levels.append(jnp.stack([jnp.concatenate([lm, 0 * lm], axis=1), jnp.concatenate([0 * lm, lm], axis=1)]))
        m *= 2
    lvl = jnp.stack(levels).astype(BF16)
    ones = jnp.ones((C, HEAD), F32)
    halfm = jnp.stack([jnp.concatenate([ones, 0 * ones], axis=1),
                       jnp.concatenate([0 * ones, ones], axis=1)]).astype(BF16)
    eye_pk = jnp.tile(jnp.eye(C, dtype=F32), (1, GROUP))
    base_pk = jnp.tile(((ti // 2 == si // 2) & (ti > si)).astype(F32), (1, GROUP))
    tri = jnp.kron(jnp.eye(sub // C, dtype=F32), jnp.tril(jnp.ones((C, C), F32))).astype(BF16)
    row = lambda t: t.reshape(1, -1).astype(F32)
    consts = [row(g), w.astype(BF16), row(b), row(mu), row(w0), wup, row(a0), aup, gup,
              row(k_k), row(k_a), row(r_k), row(lnx_g), row(lnx_b),
              bdf.astype(BF16), bdf, lvl, halfm, eye_pk, base_pk, tri, jnp.eye(GW, dtype=F32)]
    body = functools.partial(_rwkv_body, ts=ts, sub=sub, width=width)
    return pl.pallas_call(
        body,
        grid=(B, S // ts),
        in_specs=[pl.BlockSpec((None, ts, D), lambda i, j: (i, j, 0))] + [_const_spec(c.shape) for c in consts],
        out_specs=pl.BlockSpec((None, ts, width), lambda i, j: (i, j, 0)),
        out_shape=jax.ShapeDtypeStruct((B, S, width), BF16),
        scratch_shapes=[pltpu.VMEM((SUBLANES, cols), F32), pltpu.VMEM((width // GW, GW, GW), F32)],
        compiler_params=pltpu.CompilerParams(dimension_semantics=("arbitrary", "arbitrary"),
                                             vmem_limit_bytes=VMEM_LIMIT),
        name="rwkv",
    )(x, *consts)


def _mix_body(x_ref, ya_ref, g_ref, wr_ref, br_ref, cw_ref, cb_ref, wgr_ref, bgr_ref, lam_ref,
              wg_ref, bg_ref, wa_ref, wb_ref, wm_ref, o_ref, xs_ref, hc_ref,
              *, ts, sub, width, d, kconv):
    s = pl.program_id(1)
    nsub = ts // sub

    @pl.when(s == 0)
    def _():
        xs_ref[...] = jnp.zeros_like(xs_ref)
        hc_ref[...] = jnp.zeros_like(hc_ref)

    row = lax.broadcasted_iota(jnp.int32, (sub, 1), 0)
    sub8 = row % SUBLANES
    lam = lam_ref[...]
    log_sig = jnp.minimum(lam, 0.0) - jnp.log(1.0 + jnp.exp(-jnp.abs(lam)))
    hs, ybs, heads_out, carries, tails = {}, {}, {}, {}, {}

    def rg(u):
        x = x_ref[u * sub:(u + 1) * sub, :]
        h = _rms(x, g_ref[...]).astype(BF16)
        hs[u] = h
        yield
        pxy = jnp.dot(h, wr_ref[...], preferred_element_type=F32) + br_ref[...]
        px = pxy[:, :width]
        py = pxy[:, width:]
        yield
        prev8 = xs_ref[...] if u == 0 else tails[u - 1]
        tails[u] = px[sub - SUBLANES:sub, :]
        xb = cb_ref[...] + px * cw_ref[kconv - 1:kconv, :]
        for j in range(1, kconv):
            xb = xb + _shift_rows(px, prev8, j, sub8) * cw_ref[kconv - 1 - j:kconv - j, :]
        yield
        xb_bf = xb.astype(BF16)
        half = width // 2
        g0 = jnp.dot(xb_bf[:, :half], wgr_ref[0], preferred_element_type=F32)
        g1 = jnp.dot(xb_bf[:, half:], wgr_ref[1], preferred_element_type=F32)
        ga = _sigmoid(jnp.concatenate([g0[:, :half], g1[:, :half]], axis=1) + bgr_ref[:, :width])
        gx = _sigmoid(jnp.concatenate([g0[:, half:], g1[:, half:]], axis=1) + bgr_ref[:, width:])
        yield
        log_a = LRU_C * ga * log_sig
        a = jnp.exp(log_a)
        mult = _sqrt(1.0 - jnp.exp(2.0 * log_a))
        if u == 0:
            mult = jnp.where((row == 0) & (s == 0), 1.0, mult)
        uu = xb * gx * mult
        yield
        dd = 1
        while dd < SUBLANES:
            keep = sub8 >= dd
            u_sh = _group_roll(uu, dd)
            a_sh = _group_roll(a, dd)
            uu = uu + jnp.where(keep, a * u_sh, 0.0)
            a = jnp.where(keep, a * a_sh, a)
            dd *= 2
            yield
        h_prev = hc_ref[...] if u == 0 else carries[u - 1]
        gelu_y = _gelu(py)
        yield
        outs = []
        for r0 in range(0, sub, SUBLANES):
            hg = uu[r0:r0 + SUBLANES, :] + a[r0:r0 + SUBLANES, :] * h_prev
            h_prev = hg[SUBLANES - 1:SUBLANES, :]
            outs.append(hg)
        carries[u] = h_prev
        ybs[u] = (jnp.concatenate(outs, axis=0) * gelu_y).astype(BF16)

    def head(u):
        for _ in range(3):
            yield
        h = hs[u]
        half = d // 2
        parts = []
        for c0 in range(0, 2 * d, half):
            parts.append(jnp.dot(h, wg_ref[:, c0:c0 + half], preferred_element_type=F32) + bg_ref[:, c0:c0 + half])
            yield
        ya = jnp.dot(ya_ref[u * sub:(u + 1) * sub, :], wa_ref[...], preferred_element_type=F32)
        heads_out[u] = (jnp.concatenate(parts[:2], axis=1), jnp.concatenate(parts[2:], axis=1), ya)

    def tail(u):
        ga, gb, ya = heads_out[u]
        yb = jnp.dot(ybs[u], wb_ref[...], preferred_element_type=F32)
        yield
        merged = (_sigmoid(ga) * ya + _sigmoid(gb) * yb).astype(BF16)
        yield
        o_ref[u * sub:(u + 1) * sub, :] = (x_ref[u * sub:(u + 1) * sub, :]
                                           + jnp.dot(merged, wm_ref[...], preferred_element_type=F32))

    for step in range(nsub + 1):
        gens = []
        if step < nsub:
            gens += [rg(step), head(step)]
        if step >= 1:
            gens.append(tail(step - 1))
        _round_robin(gens)
    hc_ref[...] = carries[nsub - 1]
    xs_ref[...] = tails[nsub - 1]


def _mix_call(x, ya, g, wr, br, conv_w, conv_b, w_ga, b_ga, w_gx, b_gx, lam, wg, bg, wa, wb, wm, *, ts, sub=256):
    sub = min(sub, ts)
    B, S, D = x.shape
    width = lam.shape[-1]
    nb = w_ga.shape[0]
    bdiag = lambda t, lo, hi: jax.scipy.linalg.block_diag(*[t[i] for i in range(lo, hi)])
    wgr = jnp.stack([jnp.concatenate([bdiag(w_ga, lo, lo + nb // 2), bdiag(w_gx, lo, lo + nb // 2)], axis=1)
                     for lo in (0, nb // 2)]).astype(BF16)
    row = lambda t: t.reshape(1, -1).astype(F32)
    consts = [row(g), wr.astype(BF16), row(br), conv_w.astype(F32), row(conv_b), wgr,
              jnp.concatenate([row(b_ga), row(b_gx)], axis=1), row(lam),
              wg.astype(BF16), row(bg), wa.astype(BF16), wb.astype(BF16), wm.astype(BF16)]
    body = functools.partial(_mix_body, ts=ts, sub=sub, width=width, d=D, kconv=conv_w.shape[0])
    tok = lambda n: pl.BlockSpec((None, ts, n), lambda i, j: (i, j, 0))
    return pl.pallas_call(
        body,
        grid=(B, S // ts),
        in_specs=[tok(D), tok(ya.shape[-1])] + [_const_spec(c.shape) for c in consts],
        out_specs=tok(D),
        out_shape=jax.ShapeDtypeStruct((B, S, D), F32),
        scratch_shapes=[pltpu.VMEM((SUBLANES, width), F32), pltpu.VMEM((1, width), F32)],
        compiler_params=pltpu.CompilerParams(dimension_semantics=("arbitrary", "arbitrary"),
                                             vmem_limit_bytes=VMEM_LIMIT),
        name="mix",
    )(x, ya, *consts)


def _kv_body(m_ref, g_ref, w_ref, o_ref):
    h = _rms(m_ref[...], g_ref[...]).astype(BF16)
    o_ref[...] = jnp.dot(h, w_ref[...], preferred_element_type=F32).astype(o_ref.dtype)


def _kv_call(mem, g, w):
    B, M, D = mem.shape
    consts = [g.reshape(1, -1).astype(F32), w.astype(BF16)]
    return pl.pallas_call(
        _kv_body,
        grid=(B,),
        in_specs=[pl.BlockSpec((None, M, D), lambda i: (i, 0, 0))] + [_const_spec(c.shape) for c in consts],
        out_specs=pl.BlockSpec((None, M, 2 * D), lambda i: (i, 0, 0)),
        out_shape=jax.ShapeDtypeStruct((B, M, 2 * D), BF16),
        compiler_params=pltpu.CompilerParams(dimension_semantics=("arbitrary",),
                                             vmem_limit_bytes=VMEM_LIMIT),
        name="memkv",
    )(mem, *consts)


def _attn_body(x_ref, kv_ref, g_ref, wq_ref, wo_ref, o_ref, *, d, heads):
    x = x_ref[...]
    h = _rms(x, g_ref[...]).astype(BF16)
    q = jnp.dot(h, wq_ref[...], preferred_element_type=F32)
    hd = d // heads
    scale = hd ** -0.5
    hr = range(heads)
    sc = [lax.dot_general(q[:, i * hd:(i + 1) * hd].astype(BF16), kv_ref[:, i * hd:(i + 1) * hd],
                          (((1,), (1,)), ((), ())), preferred_element_type=F32) * scale for i in hr]
    e = [jnp.exp(sc[i] - jnp.max(sc[i], axis=-1, keepdims=True)) for i in hr]
    pr = [(e[i] / jnp.sum(e[i], axis=-1, keepdims=True)).astype(BF16) for i in hr]
    outs = [jnp.dot(pr[i], kv_ref[:, d + i * hd:d + (i + 1) * hd], preferred_element_type=F32).astype(BF16)
            for i in hr]
    o = jnp.concatenate(outs, axis=1)
    o_ref[...] = x + jnp.dot(o, wo_ref[...], preferred_element_type=F32)


def _attn_call(x, kv, g, wq, wo, *, tm, heads):
    B, S, D = x.shape
    M = kv.shape[1]
    consts = [g.reshape(1, -1).astype(F32), wq.astype(BF16), wo.astype(BF16)]
    tok = pl.BlockSpec((None, tm, D), lambda i, j: (i, j, 0))
    return pl.pallas_call(
        functools.partial(_attn_body, d=D, heads=heads),
        grid=(B, S // tm),
        in_specs=[tok, pl.BlockSpec((None, M, 2 * D), lambda i, j: (i, 0, 0))]
        + [_const_spec(c.shape) for c in consts],
        out_specs=tok,
        out_shape=jax.ShapeDtypeStruct((B, S, D), F32),
        compiler_params=pltpu.CompilerParams(dimension_semantics=("arbitrary", "arbitrary"),
                                             vmem_limit_bytes=VMEM_LIMIT),
        name="attn",
    )(x, kv, *consts)


def _ffn_body(x_ref, g_ref, wi_ref, cw_ref, cb_ref, wo_ref, gf_ref, o_ref, us_ref, act_ref,
              *, tm, dff, ck, kconv, final_norm):
    s = pl.program_id(1)

    @pl.when(s == 0)
    def _():
        us_ref[0:SUBLANES, :] = jnp.zeros((SUBLANES, dff), F32)

    x = x_ref[...]
    h = _rms(x, g_ref[...]).astype(BF16)
    def proj_in(c0):
        return (jnp.dot(h, wi_ref[:, c0:c0 + ck], preferred_element_type=F32),
                jnp.dot(h, wi_ref[:, dff + c0:dff + c0 + ck], preferred_element_type=F32))

    starts = list(range(0, dff, ck))
    nxt = proj_in(starts[0])
    for i, c0 in enumerate(starts):
        ug, uv = nxt
        if i + 1 < len(starts):
            nxt = proj_in(starts[i + 1])
        us_ref[SUBLANES:SUBLANES + tm, c0:c0 + ck] = ug
        gate = cb_ref[:, c0:c0 + ck]
        for j in range(kconv):
            gate = gate + (us_ref[pl.ds(SUBLANES - (kconv - 1) + j, tm), c0:c0 + ck]
                           * cw_ref[j:j + 1, c0:c0 + ck])
        us_ref[0:SUBLANES, c0:c0 + ck] = ug[tm - SUBLANES:tm, :]
        act_ref[:, c0:c0 + ck] = (_gelu(gate) * uv).astype(BF16)
    acc = x + jnp.dot(act_ref[...], wo_ref[...], preferred_element_type=F32)
    if final_norm:
        acc = _rms(acc, gf_ref[...])
    o_ref[...] = acc


def _ffn_call(x, g, wi, conv_w, conv_b, wo, gf, *, tm, ck, final_norm):
    B, S, D = x.shape
    dff = wo.shape[0]
    consts = [g.reshape(1, -1).astype(F32), wi.astype(BF16), conv_w.astype(F32),
              conv_b.reshape(1, -1).astype(F32), wo.astype(BF16), gf.reshape(1, -1).astype(F32)]
    tok = pl.BlockSpec((None, tm, D), lambda i, j: (i, j, 0))
    body = functools.partial(_ffn_body, tm=tm, dff=dff, ck=ck, kconv=conv_w.shape[0], final_norm=final_norm)
    return pl.pallas_call(
        body,
        grid=(B, S // tm),
        in_specs=[tok] + [_const_spec(c.shape) for c in consts],
        out_specs=tok,
        out_shape=jax.ShapeDtypeStruct((B, S, D), F32),
        scratch_shapes=[pltpu.VMEM((tm + SUBLANES, dff), F32), pltpu.VMEM((tm, dff), BF16)],
        compiler_params=pltpu.CompilerParams(dimension_semantics=("arbitrary", "arbitrary"),
                                             vmem_limit_bytes=VMEM_LIMIT),
        name="ffn",
    )(x, *consts)


def _pick(n, pref):
    t = min(n, pref)
    while n % t:
        t //= 2
    return t


def kernel(x, mem, norm_mix_g, w_in, b_in, mu_shift, w0, w_lora_up, a0, a_lora_up, g_lora_up, k_k, k_a, r_k, lnx_g, lnx_b, w_branch_a, conv_b_w, conv_b_b, w_rg_a, b_rg_a, w_rg_x, b_rg_x, lru_lambda, w_branch_b, w_mix_out, norm_x_g, norm_mem_g, w_cq, w_ckv, w_co, norm_ffn_g, w_ffn_in, ffn_conv_w, ffn_conv_b, w_ffn_out, norm_final_g):
    depth = w_in.shape[0]
    B, S, D = x.shape
    a_width = w0.shape[-1]
    b_width = lru_lambda.shape[-1]
    n_dec, n_aaa = w_lora_up.shape[1], a_lora_up.shape[1]
    assert n_dec + n_aaa == 128, "decay/a lora widths must fill one 128-lane slab"
    cb0 = mu_shift.shape[-1]
    cb2 = cb0 + 2 * b_width
    heads = 4
    for l in range(depth):
        wup = jnp.concatenate([w_lora_up[l], jnp.zeros((n_aaa, a_width), F32)], axis=0).astype(BF16)
        aup = jnp.concatenate([jnp.zeros((n_dec, a_width), F32), a_lora_up[l]], axis=0).astype(BF16)
        ya = _rwkv_call(x, norm_mix_g[l], w_in[l][:, :cb0], b_in[l][:cb0], mu_shift[l], w0[l], wup, a0[l], aup,
                        g_lora_up[l].astype(BF16), k_k[l], k_a[l], r_k[l], lnx_g[l], lnx_b[l], ts=_pick(S, 512))
        x = _mix_call(x, ya, norm_mix_g[l], w_in[l][:, cb0:cb2], b_in[l][cb0:cb2], conv_b_w[l], conv_b_b[l],
                      w_rg_a[l], b_rg_a[l], w_rg_x[l], b_rg_x[l], lru_lambda[l], w_in[l][:, cb2:], b_in[l][cb2:],
                      w_branch_a[l], w_branch_b[l], w_mix_out[l], ts=_pick(S, 512), sub=512)
        kv = _kv_call(mem, norm_mem_g[l], w_ckv[l])
        x = _attn_call(x, kv, norm_x_g[l], w_cq[l], w_co[l], tm=_pick(S, 1024), heads=heads)
        last = l == depth - 1
        x = _ffn_call(x, norm_ffn_g[l], w_ffn_in[l], ffn_conv_w[l], ffn_conv_b[l], w_ffn_out[l], norm_final_g,
                      tm=_pick(S, 512), ck=256, final_norm=last)
    return x
```

```python
import functools
import math

import jax
import jax.numpy as jnp
from jax import lax
from jax.experimental import pallas as pl
from jax.experimental.pallas import tpu as pltpu

F32 = jnp.float32
BF16 = jnp.bfloat16

NORM_EPS = 1e-6
LNX_EPS = 64e-5
LRU_C = 8.0
LOG2E = 1.4426950408889634
HEAD = 64
CHUNK = 64
GROUP = 4
SUBLANES = 8
VMEM_LIMIT = 56 * 1024 * 1024


def _rms(x, g):
    return x * lax.rsqrt(jnp.mean(x * x, axis=-1, keepdims=True) + NORM_EPS) * g


def _sigmoid(x):
    return 1.0 / (1.0 + jnp.exp2(x * -LOG2E))


def _gelu(x):
    c = math.sqrt(2.0 / math.pi)
    return 0.5 * x * (1.0 + jnp.tanh(c * (x + 0.044715 * (x * x * x))))


def _sqrt(x):
    return jnp.where(x > 0.0, x * lax.rsqrt(x), 0.0)


def _group_roll(z, j):
    n, lanes = z.shape
    return pltpu.roll(z.reshape(n // SUBLANES, SUBLANES, lanes), j, 1).reshape(n, lanes)


def _shift_rows(x, prev8, j, sub8):
    r = _group_roll(jnp.concatenate([prev8, x], axis=0), j)
    return jnp.where(sub8 >= j, r[SUBLANES:], r[:-SUBLANES])


def _bdot(a, b):
    return jnp.dot(a.astype(BF16), b.astype(BF16), preferred_element_type=F32)


def _split2_dot_left(m, x):
    hi = x.astype(BF16)
    lo = (x - hi.astype(F32)).astype(BF16)
    return (jnp.dot(m, hi, preferred_element_type=F32)
            + jnp.dot(m, lo, preferred_element_type=F32))


def _const_spec(shape):
    n = len(shape)
    return pl.BlockSpec(shape, lambda *_: (0,) * n)


def _round_robin(gens):
    active = list(gens)
    while active:
        alive = []
        for gen in active:
            try:
                next(gen)
                alive.append(gen)
            except StopIteration:
                pass
        active = alive


def _rwkv_body(x_ref, g_ref, w_ref, b_ref, mu_ref, w0_ref, wup_ref, a0_ref, aup_ref, gup_ref,
               kk_ref, ka_ref, rk_ref, lg_ref, lb_ref, bdm_ref, bdf_ref, lvl_ref, halfm_ref, eye_ref, base_ref,
               tri_ref, eyeg_ref, o_ref, carry_ref, p_ref, *, ts, sub, width):
    s = pl.program_id(1)
    c1, c2, c3 = width, 2 * width, 3 * width
    C = CHUNK
    GW = GROUP * HEAD
    groups = range(width // GW)
    nchunks = sub // C
    nsub = ts // sub

    @pl.when(s == 0)
    def _():
        carry_ref[...] = jnp.zeros_like(carry_ref)
        p_ref[...] = jnp.zeros_like(p_ref)

    bdm = bdm_ref[...]
    bdf = bdf_ref[...]
    eye_pk = eye_ref[...]
    base_pk = base_ref[...]
    eye_gw = eyeg_ref[...]
    nlev = lvl_ref.shape[0]
    sub8 = lax.broadcasted_iota(jnp.int32, (sub, 1), 0) % SUBLANES
    ti = lax.broadcasted_iota(jnp.int32, (C, GW), 0)
    si = lax.broadcasted_iota(jnp.int32, (C, GW), 1) % HEAD
    strict = ti > si
    incl = ti >= si

    def head_sums(t):
        return jnp.concatenate([jnp.dot(t[:, i * GW:(i + 1) * GW].astype(BF16), bdm, preferred_element_type=F32)
                                for i in groups], axis=1)

    def bd_build(t, m):
        zero = jnp.zeros((C, 128), BF16)
        blocks = []
        for hd in range(GROUP):
            col = hd * HEAD // 128
            cols = [zero] * (GW // 128)
            cols[col] = t[:, col * 128:(col + 1) * 128] * m[hd * HEAD % 128 // HEAD]
            blocks.append(jnp.concatenate(cols, axis=1))
        return jnp.concatenate(blocks, axis=0)

    halfm = (halfm_ref[0], halfm_ref[1])
    zero128 = jnp.zeros((C, 128), BF16)
    assert C == HEAD and 2 * HEAD == 128

    def bd_of(t):
        return bd_build(t, halfm)

    pro_out = {}
    last_rows = {}
    res = {}
    ys = {}
    state = {}

    def pro(u):
        h = _rms(x_ref[u * sub:(u + 1) * sub, :], g_ref[...]).astype(BF16)
        p = jnp.dot(h, w_ref[...], preferred_element_type=F32) + b_ref[...]
        yield
        first = carry_ref[...] if u == 0 else last_rows[u - 1]
        last_rows[u] = p[sub - SUBLANES:sub, :]
        prev = _shift_rows(p, first, 1, sub8)
        p = p + (prev - p) * mu_ref[...]
        r = p[:, :c1]
        k = p[:, c1:c2]
        v = p[:, c2:c3]
        lora = p[:, c3:c3 + 128]
        pg = p[:, c3 + 128:]
        yield
        w_pre = w0_ref[...] + _bdot(jnp.tanh(lora), wup_ref[...])
        ew = math.exp(-0.5) * _sigmoid(w_pre)
        yield
        a = _sigmoid(a0_ref[...] + _bdot(lora, aup_ref[...]))
        g = _bdot(_sigmoid(pg), gup_ref[...])
        yield
        kk = k * kk_ref[...]
        k2 = k * (1.0 + (a - 1.0) * ka_ref[...])
        kk = kk * jnp.minimum(lax.rsqrt(head_sums(kk * kk)), 1e12)
        yield
        bonus = head_sums(r * k2 * rk_ref[...]) * v
        yield
        cl = _split2_dot_left(tri_ref[...], ew)
        yield
        w_inv = jnp.exp(cl)
        at = -kk * jnp.exp(ew - cl)
        yield
        rt = r * jnp.exp(-cl)
        bt = (kk * a) * w_inv
        kt = k2 * w_inv
        pro_out[u] = (at, rt, bt, kt, v, cl, bonus, g, bt.T, kt.T)

    def indep(u, c, gi):
        at, rt, bt, kt, v, cl = pro_out[u][:6]
        btT, ktT = pro_out[u][8:]
        rs = slice(c * C, (c + 1) * C)
        ls = slice(gi * GW, (gi + 1) * GW)
        a_c, r_c, b_c, k_c, v_c = at[rs, ls], rt[rs, ls], bt[rs, ls], kt[rs, ls], v[rs, ls]
        lhs = jnp.concatenate([a_c, r_c], axis=0).astype(BF16)

        def bd_t(xt):
            blocks = []
            for hd in range(GROUP):
                r0 = gi * GW + hd * HEAD
                col = xt[r0:r0 + HEAD, (c // 2) * 128:(c // 2 + 1) * 128]
                if hd % 2 != c % 2:
                    col = pltpu.roll(col, HEAD, 1)
                cols = [zero128] * (GW // 128)
                cols[hd * HEAD // 128] = col.astype(BF16) * halfm[hd % 2]
                blocks.append(jnp.concatenate(cols, axis=1))
            return jnp.concatenate(blocks, axis=0)

        rhs = jnp.concatenate([bd_t(btT), bd_t(ktT)], axis=1)
        A = jnp.dot(lhs, rhs, preferred_element_type=F32)
        yield
        n = jnp.where(strict, A[:C, :GW], 0.0)
        aak = jnp.where(strict, A[:C, GW:], 0.0).astype(BF16)
        arbk = jnp.concatenate([jnp.where(incl, A[C:, :GW], 0.0), jnp.where(incl, A[C:, GW:], 0.0)],
                               axis=1).astype(BF16)
        n_bf = n.astype(BF16)
        T = eye_pk + n * base_pk
        for lv in range(nlev):
            TM = jnp.dot(T.astype(BF16), bd_build(n_bf, (lvl_ref[lv, 0], lvl_ref[lv, 1])),
                         preferred_element_type=F32)
            yield
            T = T + jnp.dot(TM.astype(BF16), bd_of(T.astype(BF16)), preferred_element_type=F32)
            yield
        v_bf = v_c.astype(BF16)
        bdv = bd_of(v_bf)
        AV = jnp.dot(aak, bdv, preferred_element_type=F32)
        yield
        wc = jnp.exp(-cl[(c + 1) * C - 1:(c + 1) * C, ls])
        hat = jnp.concatenate([b_c * wc, k_c * wc], axis=0).astype(BF16)
        wcol = jnp.sum(eye_gw * wc, axis=1, keepdims=True)
        res[(u, c, gi)] = (lhs, T.astype(BF16), AV, arbk, bdv, hat, v_bf, wcol)

    def dep(u, gi):
        P = p_ref[gi] if u == 0 else state[gi]
        for c in range(nchunks):
            lhs, T_bf, AV, arbk, bdv, hat, v_bf, wcol = res[(u, c, gi)]
            AP = jnp.dot(lhs, P.astype(BF16), preferred_element_type=F32)
            yield
            U = jnp.dot(T_bf, bd_of((AP[:C] + AV).astype(BF16)), preferred_element_type=F32)
            yield
            U_bf = U.astype(BF16)
            ys[(u, c, gi)] = AP[C:] + jnp.dot(arbk, jnp.concatenate([bd_of(U_bf), bdv], axis=0),
                                              preferred_element_type=F32)
            upd = lax.dot_general(hat, jnp.concatenate([U_bf, v_bf], axis=0), (((0,), (0,)), ((), ())),
                                  preferred_element_type=F32)
            yield
            P = P * wcol + upd * bdf
        state[gi] = P
        if u == nsub - 1:
            p_ref[gi] = P

    def epi(u):
        bonus, g = pro_out[u][6:8]
        y = jnp.concatenate([jnp.concatenate([ys[(u, c, gi)] for gi in groups], axis=1)
                             for c in range(nchunks)], axis=0)
        inv_n = 1.0 / HEAD
        mean = head_sums(y) * inv_n
        yield
        yc = y - mean
        var = head_sums(yc * yc) * inv_n
        yield
        yn = yc * lax.rsqrt(var + LNX_EPS) * lg_ref[...] + lb_ref[...]
        o_ref[u * sub:(u + 1) * sub, :] = ((yn + bonus) * g).astype(o_ref.dtype)

    for step in range(nsub + 3):
        gens = []
        if step < nsub:
            gens.append(pro(step))
        if 0 <= step - 1 < nsub:
            gens += [indep(step - 1, c, gi) for c in range(nchunks) for gi in groups]
        if 0 <= step - 2 < nsub:
            gens += [dep(step - 2, gi) for gi in groups]
        if 0 <= step - 3 < nsub:
            gens.append(epi(step - 3))
        _round_robin(gens)
    carry_ref[...] = last_rows[nsub - 1]


def _rwkv_call(x, g, w, b, mu, w0, wup, a0, aup, gup, k_k, k_a, r_k, lnx_g, lnx_b, *, ts, sub=256):
    sub = min(sub, ts)
    B, S, D = x.shape
    width = w0.shape[-1]
    cols = w.shape[-1]
    C, GW = CHUNK, GROUP * HEAD
    bdf = jnp.kron(jnp.eye(GROUP, dtype=F32), jnp.ones((HEAD, HEAD), F32))
    ti = jnp.arange(C)[:, None]
    si = jnp.arange(C)[None, :]
    levels = []
    m = 2
    while m < C:
        lm = ((ti // m) % 2 == 1) & ((si // m) % 2 == 0) & (ti // (2 * m) == si // (2 * m))
        lm = lm.astype(F32)
        levels.append(jnp.stack([jnp.concatenate([lm, 0 * lm], axis=1), jnp.concatenate([0 * lm, lm], axis=1)]))
        m *= 2
    lvl = jnp.stack(levels).astype(BF16)
    ones = jnp.ones((C, HEAD), F32)
    halfm = jnp.stack([jnp.concatenate([ones, 0 * ones], axis=1),
                       jnp.concatenate([0 * ones, ones], axis=1)]).astype(BF16)
    eye_pk = jnp.tile(jnp.eye(C, dtype=F32), (1, GROUP))
    base_pk = jnp.tile(((ti // 2 == si // 2) & (ti > si)).astype(F32), (1, GROUP))
    tri = jnp.kron(jnp.eye(sub // C, dtype=F32), jnp.tril(jnp.ones((C, C), F32))).astype(BF16)
    row = lambda t: t.reshape(1, -1).astype(F32)
    consts = [row(g), w.astype(BF16), row(b), row(mu), row(w0), wup, row(a0), aup, gup,
              row(k_k), row(k_a), row(r_k), row(lnx_g), row(lnx_b),
              bdf.astype(BF16), bdf, lvl, halfm, eye_pk, base_pk, tri, jnp.eye(GW, dtype=F32)]
    body = functools.partial(_rwkv_body, ts=ts, sub=sub, width=width)
    return pl.pallas_call(
        body,
        grid=(B, S // ts),
        in_specs=[pl.BlockSpec((None, ts, D), lambda i, j: (i, j, 0))] + [_const_spec(c.shape) for c in consts],
        out_specs=pl.BlockSpec((None, ts, width), lambda i, j: (i, j, 0)),
        out_shape=jax.ShapeDtypeStruct((B, S, width), BF16),
        scratch_shapes=[pltpu.VMEM((SUBLANES, cols), F32), pltpu.VMEM((width // GW, GW, GW), F32)],
        compiler_params=pltpu.CompilerParams(dimension_semantics=("arbitrary", "arbitrary"),
                                             vmem_limit_bytes=VMEM_LIMIT),
        name="rwkv",
    )(x, *consts)


def _mix_body(x_ref, ya_ref, g_ref, wr_ref, br_ref, cw_ref, cb_ref, wgr_ref, bgr_ref, lam_ref,
              wg_ref, bg_ref, wa_ref, wb_ref, wm_ref, o_ref, xs_ref, hc_ref,
              *, ts, sub, width, d, kconv):
    s = pl.program_id(1)
    nsub = ts // sub

    @pl.when(s == 0)
    def _():
        xs_ref[...] = jnp.zeros_like(xs_ref)
        hc_ref[...] = jnp.zeros_like(hc_ref)

    row = lax.broadcasted_iota(jnp.int32, (sub, 1), 0)
    sub8 = row % SUBLANES
    lam = lam_ref[...]
    log_sig = LRU_C * (jnp.minimum(lam, 0.0) - jnp.log(1.0 + jnp.exp(-jnp.abs(lam))))
    hs, ybs, heads_out, carries, tails = {}, {}, {}, {}, {}

    def rg(u):
        x = x_ref[u * sub:(u + 1) * sub, :]
        h = _rms(x, g_ref[...]).astype(BF16)
        hs[u] = h
        yield
        pxy = jnp.dot(h, wr_ref[...], preferred_element_type=F32) + br_ref[...]
        px = pxy[:, :width]
        py = pxy[:, width:]
        yield
        prev8 = xs_ref[...] if u == 0 else tails[u - 1]
        tails[u] = px[sub - SUBLANES:sub, :]
        xb = cb_ref[...] + px * cw_ref[kconv - 1:kconv, :]
        for j in range(1, kconv):
            xb = xb + _shift_rows(px, prev8, j, sub8) * cw_ref[kconv - 1 - j:kconv - j, :]
        yield
        xb_bf = xb.astype(BF16)
        half = width // 2
        g0 = jnp.dot(xb_bf[:, :half], wgr_ref[0], preferred_element_type=F32)
        g1 = jnp.dot(xb_bf[:, half:], wgr_ref[1], preferred_element_type=F32)
        ga = _sigmoid(jnp.concatenate([g0[:, :half], g1[:, :half]], axis=1) + bgr_ref[:, :width])
        gx = _sigmoid(jnp.concatenate([g0[:, half:], g1[:, half:]], axis=1) + bgr_ref[:, width:])
        yield
        a = jnp.exp(ga * log_sig)
        mult = _sqrt(1.0 - a * a)
        if u == 0:
            mult = jnp.where((row == 0) & (s == 0), 1.0, mult)
        uu = xb * gx * mult
        yield
        dd = 1
        while dd < SUBLANES:
            keep = sub8 >= dd
            u_sh = _group_roll(uu, dd)
            a_sh = _group_roll(a, dd)
            uu = uu + jnp.where(keep, a * u_sh, 0.0)
            a = jnp.where(keep, a * a_sh, a)
            dd *= 2
            yield
        h_prev = hc_ref[...] if u == 0 else carries[u - 1]
        gelu_y = _gelu(py)
        yield
        outs = []
        for r0 in range(0, sub, SUBLANES):
            hg = uu[r0:r0 + SUBLANES, :] + a[r0:r0 + SUBLANES, :] * h_prev
            h_prev = hg[SUBLANES - 1:SUBLANES, :]
            outs.append(hg)
        carries[u] = h_prev
        ybs[u] = (jnp.concatenate(outs, axis=0) * gelu_y).astype(BF16)

    def head(u):
        for _ in range(3):
            yield
        h = hs[u]
        half = d // 2
        parts = []
        for c0 in range(0, 2 * d, half):
            parts.append(jnp.dot(h, wg_ref[:, c0:c0 + half], preferred_element_type=F32) + bg_ref[:, c0:c0 + half])
            yield
        ya = jnp.dot(ya_ref[u * sub:(u + 1) * sub, :], wa_ref[...], preferred_element_type=F32)
        heads_out[u] = (jnp.concatenate(parts[:2], axis=1), jnp.concatenate(parts[2:], axis=1), ya)

    def tail(u):
        ga, gb, ya = heads_out[u]
        yb = jnp.dot(ybs[u], wb_ref[...], preferred_element_type=F32)
        yield
        merged = (_sigmoid(ga) * ya + _sigmoid(gb) * yb).astype(BF16)
        yield
        o_ref[u * sub:(u + 1) * sub, :] = (x_ref[u * sub:(u + 1) * sub, :]
                                           + jnp.dot(merged, wm_ref[...], preferred_element_type=F32))

    for step in range(nsub + 1):
        gens = []
        if step < nsub:
            gens += [rg(step), head(step)]
        if step >= 1:
            gens.append(tail(step - 1))
        _round_robin(gens)
    hc_ref[...] = carries[nsub - 1]
    xs_ref[...] = tails[nsub - 1]


def _mix_call(x, ya, g, wr, br, conv_w, conv_b, w_ga, b_ga, w_gx, b_gx, lam, wg, bg, wa, wb, wm, *, ts, sub=256):
    sub = min(sub, ts)
    B, S, D = x.shape
    width = lam.shape[-1]
    nb = w_ga.shape[0]
    bdiag = lambda t, lo, hi: jax.scipy.linalg.block_diag(*[t[i] for i in range(lo, hi)])
    wgr = jnp.stack([jnp.concatenate([bdiag(w_ga, lo, lo + nb // 2), bdiag(w_gx, lo, lo + nb // 2)], axis=1)
                     for lo in (0, nb // 2)]).astype(BF16)
    row = lambda t: t.reshape(1, -1).astype(F32)
    consts = [row(g), wr.astype(BF16), row(br), conv_w.astype(F32), row(conv_b), wgr,
              jnp.concatenate([row(b_ga), row(b_gx)], axis=1), row(lam),
              wg.astype(BF16), row(bg), wa.astype(BF16), wb.astype(BF16), wm.astype(BF16)]
    body = functools.partial(_mix_body, ts=ts, sub=sub, width=width, d=D, kconv=conv_w.shape[0])
    tok = lambda n: pl.BlockSpec((None, ts, n), lambda i, j: (i, j, 0))
    return pl.pallas_call(
        body,
        grid=(B, S // ts),
        in_specs=[tok(D), tok(ya.shape[-1])] + [_const_spec(c.shape) for c in consts],
        out_specs=tok(D),
        out_shape=jax.ShapeDtypeStruct((B, S, D), F32),
        scratch_shapes=[pltpu.VMEM((SUBLANES, width), F32), pltpu.VMEM((1, width), F32)],
        compiler_params=pltpu.CompilerParams(dimension_semantics=("arbitrary", "arbitrary"),
                                             vmem_limit_bytes=VMEM_LIMIT),
        name="mix",
    )(x, ya, *consts)


def _kv_body(m_ref, g_ref, w_ref, o_ref):
    h = _rms(m_ref[...], g_ref[...]).astype(BF16)
    o_ref[...] = jnp.dot(h, w_ref[...], preferred_element_type=F32).astype(o_ref.dtype)


def _kv_call(mem, g, w):
    B, M, D = mem.shape
    consts = [g.reshape(1, -1).astype(F32), w.astype(BF16)]
    return pl.pallas_call(
        _kv_body,
        grid=(B,),
        in_specs=[pl.BlockSpec((None, M, D), lambda i: (i, 0, 0))] + [_const_spec(c.shape) for c in consts],
        out_specs=pl.BlockSpec((None, M, 2 * D), lambda i: (i, 0, 0)),
        out_shape=jax.ShapeDtypeStruct((B, M, 2 * D), BF16),
        compiler_params=pltpu.CompilerParams(dimension_semantics=("arbitrary",),
                                             vmem_limit_bytes=VMEM_LIMIT),
        name="memkv",
    )(mem, *consts)


def _attn_body(x_ref, kv_ref, g_ref, wq_ref, wo_ref, o_ref, *, d, heads):
    x = x_ref[...]
    h = _rms(x, g_ref[...]).astype(BF16)
    q = jnp.dot(h, wq_ref[...], preferred_element_type=F32)
    hd = d // heads
    scale = hd ** -0.5
    hr = range(heads)
    sc = [lax.dot_general(q[:, i * hd:(i + 1) * hd].astype(BF16), kv_ref[:, i * hd:(i + 1) * hd],
                          (((1,), (1,)), ((), ())), preferred_element_type=F32) * scale for i in hr]
    e = [jnp.exp(sc[i] - jnp.max(sc[i], axis=-1, keepdims=True)) for i in hr]
    pr = [(e[i] / jnp.sum(e[i], axis=-1, keepdims=True)).astype(BF16) for i in hr]
    outs = [jnp.dot(pr[i], kv_ref[:, d + i * hd:d + (i + 1) * hd], preferred_element_type=F32).astype(BF16)
            for i in hr]
    o = jnp.concatenate(outs, axis=1)
    o_ref[...] = x + jnp.dot(o, wo_ref[...], preferred_element_type=F32)


def _attn_call(x, kv, g, wq, wo, *, tm, heads):
    B, S, D = x.shape
    M = kv.shape[1]
    consts = [g.reshape(1, -1).astype(F32), wq.astype(BF16), wo.astype(BF16)]
    tok = pl.BlockSpec((None, tm, D), lambda i, j: (i, j, 0))
    return pl.pallas_call(
        functools.partial(_attn_body, d=D, heads=heads),
        grid=(B, S // tm),
        in_specs=[tok, pl.BlockSpec((None, M, 2 * D), lambda i, j: (i, 0, 0))]
        + [_const_spec(c.shape) for c in consts],
        out_specs=tok,
        out_shape=jax.ShapeDtypeStruct((B, S, D), F32),
        compiler_params=pltpu.CompilerParams(dimension_semantics=("arbitrary", "arbitrary"),
                                             vmem_limit_bytes=VMEM_LIMIT),
        name="attn",
    )(x, kv, *consts)


def _ffn_body(x_ref, g_ref, wi_ref, cw_ref, cb_ref, wo_ref, gf_ref, o_ref, us_ref, act_ref,
              *, tm, dff, ck, kconv, final_norm):
    s = pl.program_id(1)

    @pl.when(s == 0)
    def _():
        us_ref[0:SUBLANES, :] = jnp.zeros((SUBLANES, dff), F32)

    x = x_ref[...]
    h = _rms(x, g_ref[...]).astype(BF16)
    def proj_in(c0):
        return (jnp.dot(h, wi_ref[:, c0:c0 + ck], preferred_element_type=F32),
                jnp.dot(h, wi_ref[:, dff + c0:dff + c0 + ck], preferred_element_type=F32))

    starts = list(range(0, dff, ck))
    nxt = proj_in(starts[0])
    for i, c0 in enumerate(starts):
        ug, uv = nxt
        if i + 1 < len(starts):
            nxt = proj_in(starts[i + 1])
        us_ref[SUBLANES:SUBLANES + tm, c0:c0 + ck] = ug
        gate = cb_ref[:, c0:c0 + ck]
        for j in range(kconv):
            gate = gate + (us_ref[pl.ds(SUBLANES - (kconv - 1) + j, tm), c0:c0 + ck]
                           * cw_ref[j:j + 1, c0:c0 + ck])
        us_ref[0:SUBLANES, c0:c0 + ck] = ug[tm - SUBLANES:tm, :]
        act_ref[:, c0:c0 + ck] = (_gelu(gate) * uv).astype(BF16)
    acc = x + jnp.dot(act_ref[...], wo_ref[...], preferred_element_type=F32)
    if final_norm:
        acc = _rms(acc, gf_ref[...])
    o_ref[...] = acc


def _ffn_call(x, g, wi, conv_w, conv_b, wo, gf, *, tm, ck, final_norm):
    B, S, D = x.shape
    dff = wo.shape[0]
    consts = [g.reshape(1, -1).astype(F32), wi.astype(BF16), conv_w.astype(F32),
              conv_b.reshape(1, -1).astype(F32), wo.astype(BF16), gf.reshape(1, -1).astype(F32)]
    tok = pl.BlockSpec((None, tm, D), lambda i, j: (i, j, 0))
    body = functools.partial(_ffn_body, tm=tm, dff=dff, ck=ck, kconv=conv_w.shape[0], final_norm=final_norm)
    return pl.pallas_call(
        body,
        grid=(B, S // tm),
        in_specs=[tok] + [_const_spec(c.shape) for c in consts],
        out_specs=tok,
        out_shape=jax.ShapeDtypeStruct((B, S, D), F32),
        scratch_shapes=[pltpu.VMEM((tm + SUBLANES, dff), F32), pltpu.VMEM((tm, dff), BF16)],
        compiler_params=pltpu.CompilerParams(dimension_semantics=("arbitrary", "arbitrary"),
                                             vmem_limit_bytes=VMEM_LIMIT),
        name="ffn",
    )(x, *consts)


def _pick(n, pref):
    t = min(n, pref)
    while n % t:
        t //= 2
    return t


def kernel(x, mem, norm_mix_g, w_in, b_in, mu_shift, w0, w_lora_up, a0, a_lora_up, g_lora_up, k_k, k_a, r_k, lnx_g, lnx_b, w_branch_a, conv_b_w, conv_b_b, w_rg_a, b_rg_a, w_rg_x, b_rg_x, lru_lambda, w_branch_b, w_mix_out, norm_x_g, norm_mem_g, w_cq, w_ckv, w_co, norm_ffn_g, w_ffn_in, ffn_conv_w, ffn_conv_b, w_ffn_out, norm_final_g):
    depth = w_in.shape[0]
    B, S, D = x.shape
    a_width = w0.shape[-1]
    b_width = lru_lambda.shape[-1]
    n_dec, n_aaa = w_lora_up.shape[1], a_lora_up.shape[1]
    assert n_dec + n_aaa == 128, "decay/a lora widths must fill one 128-lane slab"
    cb0 = mu_shift.shape[-1]
    cb2 = cb0 + 2 * b_width
    heads = 4
    for l in range(depth):
        wup = jnp.concatenate([w_lora_up[l], jnp.zeros((n_aaa, a_width), F32)], axis=0).astype(BF16)
        aup = jnp.concatenate([jnp.zeros((n_dec, a_width), F32), a_lora_up[l]], axis=0).astype(BF16)
        ya = _rwkv_call(x, norm_mix_g[l], w_in[l][:, :cb0], b_in[l][:cb0], mu_shift[l], w0[l], wup, a0[l], aup,
                        g_lora_up[l].astype(BF16), k_k[l], k_a[l], r_k[l], lnx_g[l], lnx_b[l], ts=_pick(S, 512))
        x = _mix_call(x, ya, norm_mix_g[l], w_in[l][:, cb0:cb2], b_in[l][cb0:cb2], conv_b_w[l], conv_b_b[l],
                      w_rg_a[l], b_rg_a[l], w_rg_x[l], b_rg_x[l], lru_lambda[l], w_in[l][:, cb2:], b_in[l][cb2:],
                      w_branch_a[l], w_branch_b[l], w_mix_out[l], ts=_pick(S, 512), sub=512)
        kv = _kv_call(mem, norm_mem_g[l], w_ckv[l])
        x = _attn_call(x, kv, norm_x_g[l], w_cq[l], w_co[l], tm=_pick(S, 1024), heads=heads)
        last = l == depth - 1
        x = _ffn_call(x, norm_ffn_g[l], w_ffn_in[l], ffn_conv_w[l], ffn_conv_b[l], w_ffn_out[l], norm_final_g,
                      tm=_pick(S, 512), ck=256, final_norm=last)
    return x
```

```python
import functools
import math

import jax
import jax.numpy as jnp
from jax import lax
from jax.experimental import pallas as pl
from jax.experimental.pallas import tpu as pltpu

F32 = jnp.float32
BF16 = jnp.bfloat16

NORM_EPS = 1e-6
LNX_EPS = 64e-5
LRU_C = 8.0
LOG2E = 1.4426950408889634
HEAD = 64
CHUNK = 64
GROUP = 4
SUBLANES = 8
VMEM_LIMIT = 56 * 1024 * 1024


def _rms(x, g):
    return x * lax.rsqrt(jnp.mean(x * x, axis=-1, keepdims=True) + NORM_EPS) * g


def _sigmoid(x):
    return 1.0 / (1.0 + jnp.exp2(x * -LOG2E))


def _gelu(x):
    c = math.sqrt(2.0 / math.pi)
    return 0.5 * x * (1.0 + jnp.tanh(c * (x + 0.044715 * (x * x * x))))


def _sqrt(x):
    return jnp.where(x > 0.0, x * lax.rsqrt(x), 0.0)


def _group_roll(z, j):
    n, lanes = z.shape
    return pltpu.roll(z.reshape(n // SUBLANES, SUBLANES, lanes), j, 1).reshape(n, lanes)


def _shift_rows(x, prev8, j, sub8):
    r = _group_roll(jnp.concatenate([prev8, x], axis=0), j)
    return jnp.where(sub8 >= j, r[SUBLANES:], r[:-SUBLANES])


def _bdot(a, b):
    return jnp.dot(a.astype(BF16), b.astype(BF16), preferred_element_type=F32)


def _split2_dot_left(m, x):
    hi = x.astype(BF16)
    lo = (x - hi.astype(F32)).astype(BF16)
    return (jnp.dot(m, hi, preferred_element_type=F32)
            + jnp.dot(m, lo, preferred_element_type=F32))


def _const_spec(shape):
    n = len(shape)
    return pl.BlockSpec(shape, lambda *_: (0,) * n)


def _round_robin(gens):
    active = list(gens)
    while active:
        alive = []
        for gen in active:
            try:
                next(gen)
                alive.append(gen)
            except StopIteration:
                pass
        active = alive


def _rwkv_body(x_ref, g_ref, w_ref, b_ref, mu_ref, w0_ref, wup_ref, a0_ref, aup_ref, gup_ref,
               kk_ref, ka_ref, rk_ref, lg_ref, lb_ref, bdm_ref, lvl_ref, halfm_ref, eye_ref, base_ref,
               tri_ref, eyeg_ref, o_ref, carry_ref, p_ref, *, ts, sub, width):
    s = pl.program_id(1)
    c1, c2, c3 = width, 2 * width, 3 * width
    C = CHUNK
    GW = GROUP * HEAD
    groups = range(width // GW)
    nchunks = sub // C
    nsub = ts // sub

    @pl.when(s == 0)
    def _():
        carry_ref[...] = jnp.zeros_like(carry_ref)
        p_ref[...] = jnp.zeros_like(p_ref)

    bdm = bdm_ref[...]
    eye_pk = eye_ref[...]
    base_pk = base_ref[...]
    eye_gw = eyeg_ref[...]
    nlev = lvl_ref.shape[0]
    sub8 = lax.broadcasted_iota(jnp.int32, (sub, 1), 0) % SUBLANES
    ti = lax.broadcasted_iota(jnp.int32, (C, GW), 0)
    si = lax.broadcasted_iota(jnp.int32, (C, GW), 1) % HEAD
    strict = ti > si
    incl = ti >= si

    def head_sums(t):
        return jnp.concatenate([jnp.dot(t[:, i * GW:(i + 1) * GW].astype(BF16), bdm, preferred_element_type=F32)
                                for i in groups], axis=1)

    def bd_build(t, m):
        zero = jnp.zeros((C, 128), BF16)
        blocks = []
        for hd in range(GROUP):
            col = hd * HEAD // 128
            cols = [zero] * (GW // 128)
            cols[col] = t[:, col * 128:(col + 1) * 128] * m[hd * HEAD % 128 // HEAD]
            blocks.append(jnp.concatenate(cols, axis=1))
        return jnp.concatenate(blocks, axis=0)

    halfm = (halfm_ref[0], halfm_ref[1])
    zero128 = jnp.zeros((C, 128), BF16)
    halff = (halfm[0].astype(F32), halfm[1].astype(F32))
    assert C == HEAD and 2 * HEAD == 128

    def bd_of(t):
        return bd_build(t, halfm)

    pro_out = {}
    last_rows = {}
    res = {}
    ys = {}
    state = {}

    def pro(u):
        h = _rms(x_ref[u * sub:(u + 1) * sub, :], g_ref[...]).astype(BF16)
        p = jnp.dot(h, w_ref[...], preferred_element_type=F32) + b_ref[...]
        yield
        first = carry_ref[...] if u == 0 else last_rows[u - 1]
        last_rows[u] = p[sub - SUBLANES:sub, :]
        prev = _shift_rows(p, first, 1, sub8)
        p = p + (prev - p) * mu_ref[...]
        r = p[:, :c1]
        k = p[:, c1:c2]
        v = p[:, c2:c3]
        lora = p[:, c3:c3 + 128]
        pg = p[:, c3 + 128:]
        yield
        w_pre = w0_ref[...] + _bdot(jnp.tanh(lora), wup_ref[...])
        ew = math.exp(-0.5) * _sigmoid(w_pre)
        yield
        a = _sigmoid(a0_ref[...] + _bdot(lora, aup_ref[...]))
        g = _bdot(_sigmoid(pg), gup_ref[...])
        yield
        kk = k * kk_ref[...]
        k2 = k * (1.0 + (a - 1.0) * ka_ref[...])
        kk = kk * jnp.minimum(lax.rsqrt(head_sums(kk * kk)), 1e12)
        yield
        bonus = head_sums(r * k2 * rk_ref[...]) * v
        yield
        cl = _split2_dot_left(tri_ref[...], ew)
        yield
        w_inv = jnp.exp(cl)
        at = -kk * jnp.exp(ew - cl)
        yield
        rt = r * jnp.exp(-cl)
        bt = (kk * a) * w_inv
        kt = k2 * w_inv
        pro_out[u] = (at, rt, bt, kt, v, cl, bonus, g, bt.T, kt.T)

    def indep(u, c, gi):
        at, rt, bt, kt, v, cl = pro_out[u][:6]
        btT, ktT = pro_out[u][8:]
        rs = slice(c * C, (c + 1) * C)
        ls = slice(gi * GW, (gi + 1) * GW)
        a_c, r_c, b_c, k_c, v_c = at[rs, ls], rt[rs, ls], bt[rs, ls], kt[rs, ls], v[rs, ls]
        lhs = jnp.concatenate([a_c, r_c], axis=0).astype(BF16)

        def bd_t(xt):
            blocks = []
            for hd in range(GROUP):
                r0 = gi * GW + hd * HEAD
                col = xt[r0:r0 + HEAD, (c // 2) * 128:(c // 2 + 1) * 128]
                if hd % 2 != c % 2:
                    col = pltpu.roll(col, HEAD, 1)
                cols = [zero128] * (GW // 128)
                cols[hd * HEAD // 128] = col.astype(BF16) * halfm[hd % 2]
                blocks.append(jnp.concatenate(cols, axis=1))
            return jnp.concatenate(blocks, axis=0)

        rhs = jnp.concatenate([bd_t(btT), bd_t(ktT)], axis=1)
        A = jnp.dot(lhs, rhs, preferred_element_type=F32)
        yield
        n = jnp.where(strict, A[:C, :GW], 0.0)
        aak = jnp.where(strict, A[:C, GW:], 0.0).astype(BF16)
        arbk = jnp.concatenate([jnp.where(incl, A[C:, :GW], 0.0), jnp.where(incl, A[C:, GW:], 0.0)],
                               axis=1).astype(BF16)
        n_bf = n.astype(BF16)
        T = eye_pk + n * base_pk
        for lv in range(nlev):
            TM = jnp.dot(T.astype(BF16), bd_build(n_bf, (lvl_ref[lv, 0], lvl_ref[lv, 1])),
                         preferred_element_type=F32)
            yield
            T = T + jnp.dot(TM.astype(BF16), bd_of(T.astype(BF16)), preferred_element_type=F32)
            yield
        v_bf = v_c.astype(BF16)
        bdv = bd_of(v_bf)
        AV = jnp.dot(aak, bdv, preferred_element_type=F32)
        yield
        wc = jnp.exp(-cl[(c + 1) * C - 1:(c + 1) * C, ls])
        hat = jnp.concatenate([b_c * wc, k_c * wc], axis=0).astype(BF16)
        wcol = jnp.sum(eye_gw * wc, axis=1, keepdims=True)
        res[(u, c, gi)] = (lhs, T.astype(BF16), AV, arbk, bdv, hat, v_bf, wcol)

    def expand(pc):
        blocks = []
        for hd in range(GROUP):
            cols = [zero128] * (GW // 128)
            cols[hd * HEAD // 128] = pc[hd * HEAD:(hd + 1) * HEAD, :]
            blocks.append(jnp.concatenate(cols, axis=1))
        return jnp.concatenate(blocks, axis=0)

    def compact(m):
        return jnp.concatenate([m[hd * HEAD:(hd + 1) * HEAD, (hd * HEAD // 128) * 128:(hd * HEAD // 128 + 1) * 128]
                                * halff[hd % 2] for hd in range(GROUP)], axis=0)

    def dep(u, gi):
        P = p_ref[gi] if u == 0 else state[gi]
        for c in range(nchunks):
            lhs, T_bf, AV, arbk, bdv, hat, v_bf, wcol = res[(u, c, gi)]
            AP = jnp.dot(lhs, expand(P.astype(BF16)), preferred_element_type=F32)
            yield
            U = jnp.dot(T_bf, bd_of((AP[:C] + AV).astype(BF16)), preferred_element_type=F32)
            yield
            U_bf = U.astype(BF16)
            ys[(u, c, gi)] = AP[C:] + jnp.dot(arbk, jnp.concatenate([bd_of(U_bf), bdv], axis=0),
                                              preferred_element_type=F32)
            upd = lax.dot_general(hat, jnp.concatenate([U_bf, v_bf], axis=0), (((0,), (0,)), ((), ())),
                                  preferred_element_type=F32)
            yield
            P = P * wcol + compact(upd)
        state[gi] = P
        if u == nsub - 1:
            p_ref[gi] = P

    def epi(u):
        bonus, g = pro_out[u][6:8]
        y = jnp.concatenate([jnp.concatenate([ys[(u, c, gi)] for gi in groups], axis=1)
                             for c in range(nchunks)], axis=0)
        inv_n = 1.0 / HEAD
        mean = head_sums(y) * inv_n
        yield
        yc = y - mean
        var = head_sums(yc * yc) * inv_n
        yield
        yn = yc * lax.rsqrt(var + LNX_EPS) * lg_ref[...] + lb_ref[...]
        o_ref[u * sub:(u + 1) * sub, :] = ((yn + bonus) * g).astype(o_ref.dtype)

    for step in range(nsub + 3):
        gens = []
        if step < nsub:
            gens.append(pro(step))
        if 0 <= step - 1 < nsub:
            gens += [indep(step - 1, c, gi) for c in range(nchunks) for gi in groups]
        if 0 <= step - 2 < nsub:
            gens += [dep(step - 2, gi) for gi in groups]
        if 0 <= step - 3 < nsub:
            gens.append(epi(step - 3))
        _round_robin(gens)
    carry_ref[...] = last_rows[nsub - 1]


def _rwkv_call(x, g, w, b, mu, w0, wup, a0, aup, gup, k_k, k_a, r_k, lnx_g, lnx_b, *, ts, sub=256):
    sub = min(sub, ts)
    B, S, D = x.shape
    width = w0.shape[-1]
    cols = w.shape[-1]
    C, GW = CHUNK, GROUP * HEAD
    bdf = jnp.kron(jnp.eye(GROUP, dtype=F32), jnp.ones((HEAD, HEAD), F32))
    ti = jnp.arange(C)[:, None]
    si = jnp.arange(C)[None, :]
    levels = []
    m = 2
    while m < C:
        lm = ((ti // m) % 2 == 1) & ((si // m) % 2 == 0) & (ti // (2 * m) == si // (2 * m))
        lm = lm.astype(F32)
        levels.append(jnp.stack([jnp.concatenate([lm, 0 * lm], axis=1), jnp.concatenate([0 * lm, lm], axis=1)]))
        m *= 2
    lvl = jnp.stack(levels).astype(BF16)
    ones = jnp.ones((C, HEAD), F32)
    halfm = jnp.stack([jnp.concatenate([ones, 0 * ones], axis=1),
                       jnp.concatenate([0 * ones, ones], axis=1)]).astype(BF16)
    eye_pk = jnp.tile(jnp.eye(C, dtype=F32), (1, GROUP))
    base_pk = jnp.tile(((ti // 2 == si // 2) & (ti > si)).astype(F32), (1, GROUP))
    tri = jnp.kron(jnp.eye(sub // C, dtype=F32), jnp.tril(jnp.ones((C, C), F32))).astype(BF16)
    row = lambda t: t.reshape(1, -1).astype(F32)
    consts = [row(g), w.astype(BF16), row(b), row(mu), row(w0), wup, row(a0), aup, gup,
              row(k_k), row(k_a), row(r_k), row(lnx_g), row(lnx_b),
              bdf.astype(BF16), lvl, halfm, eye_pk, base_pk, tri, jnp.eye(GW, dtype=F32)]
    body = functools.partial(_rwkv_body, ts=ts, sub=sub, width=width)
    return pl.pallas_call(
        body,
        grid=(B, S // ts),
        in_specs=[pl.BlockSpec((None, ts, D), lambda i, j: (i, j, 0))] + [_const_spec(c.shape) for c in consts],
        out_specs=pl.BlockSpec((None, ts, width), lambda i, j: (i, j, 0)),
        out_shape=jax.ShapeDtypeStruct((B, S, width), BF16),
        scratch_shapes=[pltpu.VMEM((SUBLANES, cols), F32), pltpu.VMEM((width // GW, GW, 128), F32)],
        compiler_params=pltpu.CompilerParams(dimension_semantics=("arbitrary", "arbitrary"),
                                             vmem_limit_bytes=VMEM_LIMIT),
        name="rwkv",
    )(x, *consts)


def _mix_body(x_ref, ya_ref, g_ref, wr_ref, br_ref, cw_ref, cb_ref, wgr_ref, bgr_ref, lam_ref,
              wg_ref, bg_ref, wa_ref, wb_ref, wm_ref, o_ref, xs_ref, hc_ref,
              *, ts, sub, width, d, kconv):
    s = pl.program_id(1)
    nsub = ts // sub

    @pl.when(s == 0)
    def _():
        xs_ref[...] = jnp.zeros_like(xs_ref)
        hc_ref[...] = jnp.zeros_like(hc_ref)

    row = lax.broadcasted_iota(jnp.int32, (sub, 1), 0)
    sub8 = row % SUBLANES
    lam = lam_ref[...]
    log_sig = LRU_C * (jnp.minimum(lam, 0.0) - jnp.log(1.0 + jnp.exp(-jnp.abs(lam))))
    hs, ybs, heads_out, carries, tails = {}, {}, {}, {}, {}

    def rg(u):
        x = x_ref[u * sub:(u + 1) * sub, :]
        h = _rms(x, g_ref[...]).astype(BF16)
        hs[u] = h
        yield
        pxy = jnp.dot(h, wr_ref[...], preferred_element_type=F32) + br_ref[...]
        px = pxy[:, :width]
        py = pxy[:, width:]
        yield
        prev8 = xs_ref[...] if u == 0 else tails[u - 1]
        tails[u] = px[sub - SUBLANES:sub, :]
        xb = cb_ref[...] + px * cw_ref[kconv - 1:kconv, :]
        for j in range(1, kconv):
            xb = xb + _shift_rows(px, prev8, j, sub8) * cw_ref[kconv - 1 - j:kconv - j, :]
        yield
        xb_bf = xb.astype(BF16)
        half = width // 2
        g0 = jnp.dot(xb_bf[:, :half], wgr_ref[0], preferred_element_type=F32)
        g1 = jnp.dot(xb_bf[:, half:], wgr_ref[1], preferred_element_type=F32)
        ga = _sigmoid(jnp.concatenate([g0[:, :half], g1[:, :half]], axis=1) + bgr_ref[:, :width])
        gx = _sigmoid(jnp.concatenate([g0[:, half:], g1[:, half:]], axis=1) + bgr_ref[:, width:])
        yield
        a = jnp.exp(ga * log_sig)
        mult = _sqrt(1.0 - a * a)
        if u == 0:
            mult = jnp.where((row == 0) & (s == 0), 1.0, mult)
        uu = xb * gx * mult
        yield
        dd = 1
        while dd < SUBLANES:
            keep = sub8 >= dd
            u_sh = _group_roll(uu, dd)
            a_sh = _group_roll(a, dd)
            uu = uu + jnp.where(keep, a * u_sh, 0.0)
            a = jnp.where(keep, a * a_sh, a)
            dd *= 2
            yield
        h_prev = hc_ref[...] if u == 0 else carries[u - 1]
        gelu_y = _gelu(py)
        yield
        outs = []
        for r0 in range(0, sub, SUBLANES):
            hg = uu[r0:r0 + SUBLANES, :] + a[r0:r0 + SUBLANES, :] * h_prev
            h_prev = hg[SUBLANES - 1:SUBLANES, :]
            outs.append(hg)
        carries[u] = h_prev
        ybs[u] = (jnp.concatenate(outs, axis=0) * gelu_y).astype(BF16)

    def head(u):
        for _ in range(3):
            yield
        h = hs[u]
        half = d // 2
        parts = []
        for c0 in range(0, 2 * d, half):
            parts.append(jnp.dot(h, wg_ref[:, c0:c0 + half], preferred_element_type=F32) + bg_ref[:, c0:c0 + half])
            yield
        ya = jnp.dot(ya_ref[u * sub:(u + 1) * sub, :], wa_ref[...], preferred_element_type=F32)
        heads_out[u] = (jnp.concatenate(parts[:2], axis=1), jnp.concatenate(parts[2:], axis=1), ya)

    def tail(u):
        ga, gb, ya = heads_out[u]
        yb = jnp.dot(ybs[u], wb_ref[...], preferred_element_type=F32)
        yield
        merged = (_sigmoid(ga) * ya + _sigmoid(gb) * yb).astype(BF16)
        yield
        o_ref[u * sub:(u + 1) * sub, :] = (x_ref[u * sub:(u + 1) * sub, :]
                                           + jnp.dot(merged, wm_ref[...], preferred_element_type=F32))

    for step in range(nsub + 1):
        gens = []
        if step < nsub:
            gens += [rg(step), head(step)]
        if step >= 1:
            gens.append(tail(step - 1))
        _round_robin(gens)
    hc_ref[...] = carries[nsub - 1]
    xs_ref[...] = tails[nsub - 1]


def _mix_call(x, ya, g, wr, br, conv_w, conv_b, w_ga, b_ga, w_gx, b_gx, lam, wg, bg, wa, wb, wm, *, ts, sub=256):
    sub = min(sub, ts)
    B, S, D = x.shape
    width = lam.shape[-1]
    nb = w_ga.shape[0]
    bdiag = lambda t, lo, hi: jax.scipy.linalg.block_diag(*[t[i] for i in range(lo, hi)])
    wgr = jnp.stack([jnp.concatenate([bdiag(w_ga, lo, lo + nb // 2), bdiag(w_gx, lo, lo + nb // 2)], axis=1)
                     for lo in (0, nb // 2)]).astype(BF16)
    row = lambda t: t.reshape(1, -1).astype(F32)
    consts = [row(g), wr.astype(BF16), row(br), conv_w.astype(F32), row(conv_b), wgr,
              jnp.concatenate([row(b_ga), row(b_gx)], axis=1), row(lam),
              wg.astype(BF16), row(bg), wa.astype(BF16), wb.astype(BF16), wm.astype(BF16)]
    body = functools.partial(_mix_body, ts=ts, sub=sub, width=width, d=D, kconv=conv_w.shape[0])
    tok = lambda n: pl.BlockSpec((None, ts, n), lambda i, j: (i, j, 0))
    return pl.pallas_call(
        body,
        grid=(B, S // ts),
        in_specs=[tok(D), tok(ya.shape[-1])] + [_const_spec(c.shape) for c in consts],
        out_specs=tok(D),
        out_shape=jax.ShapeDtypeStruct((B, S, D), F32),
        scratch_shapes=[pltpu.VMEM((SUBLANES, width), F32), pltpu.VMEM((1, width), F32)],
        compiler_params=pltpu.CompilerParams(dimension_semantics=("arbitrary", "arbitrary"),
                                             vmem_limit_bytes=VMEM_LIMIT),
        name="mix",
    )(x, ya, *consts)


def _kv_body(m_ref, g_ref, w_ref, o_ref):
    h = _rms(m_ref[...], g_ref[...]).astype(BF16)
    o_ref[...] = jnp.dot(h, w_ref[...], preferred_element_type=F32).astype(o_ref.dtype)


def _kv_call(mem, g, w):
    B, M, D = mem.shape
    consts = [g.reshape(1, -1).astype(F32), w.astype(BF16)]
    return pl.pallas_call(
        _kv_body,
        grid=(B,),
        in_specs=[pl.BlockSpec((None, M, D), lambda i: (i, 0, 0))] + [_const_spec(c.shape) for c in consts],
        out_specs=pl.BlockSpec((None, M, 2 * D), lambda i: (i, 0, 0)),
        out_shape=jax.ShapeDtypeStruct((B, M, 2 * D), BF16),
        compiler_params=pltpu.CompilerParams(dimension_semantics=("arbitrary",),
                                             vmem_limit_bytes=VMEM_LIMIT),
        name="memkv",
    )(mem, *consts)


def _attn_body(x_ref, kv_ref, g_ref, wq_ref, wo_ref, o_ref, *, d, heads):
    x = x_ref[...]
    h = _rms(x, g_ref[...]).astype(BF16)
    q = jnp.dot(h, wq_ref[...], preferred_element_type=F32)
    hd = d // heads
    scale = hd ** -0.5
    hr = range(heads)
    sc = [lax.dot_general(q[:, i * hd:(i + 1) * hd].astype(BF16), kv_ref[:, i * hd:(i + 1) * hd],
                          (((1,), (1,)), ((), ())), preferred_element_type=F32) * scale for i in hr]
    e = [jnp.exp(sc[i] - jnp.max(sc[i], axis=-1, keepdims=True)) for i in hr]
    pr = [(e[i] / jnp.sum(e[i], axis=-1, keepdims=True)).astype(BF16) for i in hr]
    outs = [jnp.dot(pr[i], kv_ref[:, d + i * hd:d + (i + 1) * hd], preferred_element_type=F32).astype(BF16)
            for i in hr]
    o = jnp.concatenate(outs, axis=1)
    o_ref[...] = x + jnp.dot(o, wo_ref[...], preferred_element_type=F32)


def _attn_call(x, kv, g, wq, wo, *, tm, heads):
    B, S, D = x.shape
    M = kv.shape[1]
    consts = [g.reshape(1, -1).astype(F32), wq.astype(BF16), wo.astype(BF16)]
    tok = pl.BlockSpec((None, tm, D), lambda i, j: (i, j, 0))
    return pl.pallas_call(
        functools.partial(_attn_body, d=D, heads=heads),
        grid=(B, S // tm),
        in_specs=[tok, pl.BlockSpec((None, M, 2 * D), lambda i, j: (i, 0, 0))]
        + [_const_spec(c.shape) for c in consts],
        out_specs=tok,
        out_shape=jax.ShapeDtypeStruct((B, S, D), F32),
        compiler_params=pltpu.CompilerParams(dimension_semantics=("arbitrary", "arbitrary"),
                                             vmem_limit_bytes=VMEM_LIMIT),
        name="attn",
    )(x, kv, *consts)


def _ffn_body(x_ref, g_ref, wi_ref, cw_ref, cb_ref, wo_ref, gf_ref, o_ref, us_ref, act_ref,
              *, tm, dff, ck, kconv, final_norm):
    s = pl.program_id(1)

    @pl.when(s == 0)
    def _():
        us_ref[0:SUBLANES, :] = jnp.zeros((SUBLANES, dff), F32)

    x = x_ref[...]
    h = _rms(x, g_ref[...]).astype(BF16)
    def proj_in(c0):
        return (jnp.dot(h, wi_ref[:, c0:c0 + ck], preferred_element_type=F32),
                jnp.dot(h, wi_ref[:, dff + c0:dff + c0 + ck], preferred_element_type=F32))

    starts = list(range(0, dff, ck))
    nxt = proj_in(starts[0])
    for i, c0 in enumerate(starts):
        ug, uv = nxt
        if i + 1 < len(starts):
            nxt = proj_in(starts[i + 1])
        us_ref[SUBLANES:SUBLANES + tm, c0:c0 + ck] = ug
        gate = cb_ref[:, c0:c0 + ck]
        for j in range(kconv):
            gate = gate + (us_ref[pl.ds(SUBLANES - (kconv - 1) + j, tm), c0:c0 + ck]
                           * cw_ref[j:j + 1, c0:c0 + ck])
        us_ref[0:SUBLANES, c0:c0 + ck] = ug[tm - SUBLANES:tm, :]
        act_ref[:, c0:c0 + ck] = (_gelu(gate) * uv).astype(BF16)
    acc = x + jnp.dot(act_ref[...], wo_ref[...], preferred_element_type=F32)
    if final_norm:
        acc = _rms(acc, gf_ref[...])
    o_ref[...] = acc


def _ffn_call(x, g, wi, conv_w, conv_b, wo, gf, *, tm, ck, final_norm):
    B, S, D = x.shape
    dff = wo.shape[0]
    consts = [g.reshape(1, -1).astype(F32), wi.astype(BF16), conv_w.astype(F32),
              conv_b.reshape(1, -1).astype(F32), wo.astype(BF16), gf.reshape(1, -1).astype(F32)]
    tok = pl.BlockSpec((None, tm, D), lambda i, j: (i, j, 0))
    body = functools.partial(_ffn_body, tm=tm, dff=dff, ck=ck, kconv=conv_w.shape[0], final_norm=final_norm)
    return pl.pallas_call(
        body,
        grid=(B, S // tm),
        in_specs=[tok] + [_const_spec(c.shape) for c in consts],
        out_specs=tok,
        out_shape=jax.ShapeDtypeStruct((B, S, D), F32),
        scratch_shapes=[pltpu.VMEM((tm + SUBLANES, dff), F32), pltpu.VMEM((tm, dff), BF16)],
        compiler_params=pltpu.CompilerParams(dimension_semantics=("arbitrary", "arbitrary"),
                                             vmem_limit_bytes=VMEM_LIMIT),
        name="ffn",
    )(x, *consts)


def _pick(n, pref):
    t = min(n, pref)
    while n % t:
        t //= 2
    return t


def kernel(x, mem, norm_mix_g, w_in, b_in, mu_shift, w0, w_lora_up, a0, a_lora_up, g_lora_up, k_k, k_a, r_k, lnx_g, lnx_b, w_branch_a, conv_b_w, conv_b_b, w_rg_a, b_rg_a, w_rg_x, b_rg_x, lru_lambda, w_branch_b, w_mix_out, norm_x_g, norm_mem_g, w_cq, w_ckv, w_co, norm_ffn_g, w_ffn_in, ffn_conv_w, ffn_conv_b, w_ffn_out, norm_final_g):
    depth = w_in.shape[0]
    B, S, D = x.shape
    a_width = w0.shape[-1]
    b_width = lru_lambda.shape[-1]
    n_dec, n_aaa = w_lora_up.shape[1], a_lora_up.shape[1]
    assert n_dec + n_aaa == 128, "decay/a lora widths must fill one 128-lane slab"
    cb0 = mu_shift.shape[-1]
    cb2 = cb0 + 2 * b_width
    heads = 4
    for l in range(depth):
        wup = jnp.concatenate([w_lora_up[l], jnp.zeros((n_aaa, a_width), F32)], axis=0).astype(BF16)
        aup = jnp.concatenate([jnp.zeros((n_dec, a_width), F32), a_lora_up[l]], axis=0).astype(BF16)
        ya = _rwkv_call(x, norm_mix_g[l], w_in[l][:, :cb0], b_in[l][:cb0], mu_shift[l], w0[l], wup, a0[l], aup,
                        g_lora_up[l].astype(BF16), k_k[l], k_a[l], r_k[l], lnx_g[l], lnx_b[l], ts=_pick(S, 512))
        x = _mix_call(x, ya, norm_mix_g[l], w_in[l][:, cb0:cb2], b_in[l][cb0:cb2], conv_b_w[l], conv_b_b[l],
                      w_rg_a[l], b_rg_a[l], w_rg_x[l], b_rg_x[l], lru_lambda[l], w_in[l][:, cb2:], b_in[l][cb2:],
                      w_branch_a[l], w_branch_b[l], w_mix_out[l], ts=_pick(S, 512), sub=512)
        kv = _kv_call(mem, norm_mem_g[l], w_ckv[l])
        x = _attn_call(x, kv, norm_x_g[l], w_cq[l], w_co[l], tm=_pick(S, 1024), heads=heads)
        last = l == depth - 1
        x = _ffn_call(x, norm_ffn_g[l], w_ffn_in[l], ffn_conv_w[l], ffn_conv_b[l], w_ffn_out[l], norm_final_g,
                      tm=_pick(S, 512), ck=256, final_norm=last)
    return x
```

```python
import functools
import math

import jax
import jax.numpy as jnp
from jax import lax
from jax.experimental import pallas as pl
from jax.experimental.pallas import tpu as pltpu

F32 = jnp.float32
BF16 = jnp.bfloat16

NORM_EPS = 1e-6
LNX_EPS = 64e-5
LRU_C = 8.0
LOG2E = 1.4426950408889634
HEAD = 64
CHUNK = 64
GROUP = 4
SUBLANES = 8
VMEM_LIMIT = 56 * 1024 * 1024


def _rms(x, g):
    return x * lax.rsqrt(jnp.mean(x * x, axis=-1, keepdims=True) + NORM_EPS) * g


def _sigmoid(x):
    return 1.0 / (1.0 + jnp.exp2(x * -LOG2E))


def _gelu(x):
    c = math.sqrt(2.0 / math.pi)
    return 0.5 * x * (1.0 + jnp.tanh(c * (x + 0.044715 * (x * x * x))))


def _sqrt(x):
    return jnp.where(x > 0.0, x * lax.rsqrt(x), 0.0)


def _group_roll(z, j):
    n, lanes = z.shape
    return pltpu.roll(z.reshape(n // SUBLANES, SUBLANES, lanes), j, 1).reshape(n, lanes)


def _shift_rows(x, prev8, j, sub8):
    r = _group_roll(jnp.concatenate([prev8, x], axis=0), j)
    return jnp.where(sub8 >= j, r[SUBLANES:], r[:-SUBLANES])


def _bdot(a, b):
    return jnp.dot(a.astype(BF16), b.astype(BF16), preferred_element_type=F32)


def _split2_dot_left(m, x):
    hi = x.astype(BF16)
    lo = (x - hi.astype(F32)).astype(BF16)
    return (jnp.dot(m, hi, preferred_element_type=F32)
            + jnp.dot(m, lo, preferred_element_type=F32))


def _const_spec(shape):
    n = len(shape)
    return pl.BlockSpec(shape, lambda *_: (0,) * n)


def _round_robin(gens):
    active = list(gens)
    while active:
        alive = []
        for gen in active:
            try:
                next(gen)
                alive.append(gen)
            except StopIteration:
                pass
        active = alive


def _rwkv_body(x_ref, g_ref, w_ref, b_ref, mu_ref, w0_ref, wup_ref, a0_ref, aup_ref, gup_ref,
               kk_ref, ka_ref, rk_ref, lg_ref, lb_ref, bdm_ref, lvl_ref, halfm_ref, eye_ref, base_ref,
               tri_ref, eyeg_ref, o_ref, carry_ref, p_ref, *, ts, sub, width):
    s = pl.program_id(1)
    c1, c2, c3 = width, 2 * width, 3 * width
    C = CHUNK
    GW = GROUP * HEAD
    groups = range(width // GW)
    nchunks = sub // C
    nsub = ts // sub

    @pl.when(s == 0)
    def _():
        carry_ref[...] = jnp.zeros_like(carry_ref)
        p_ref[...] = jnp.zeros_like(p_ref)

    bdm = bdm_ref[...]
    eye_pk = eye_ref[...]
    base_pk = base_ref[...]
    eye_gw = eyeg_ref[...]
    nlev = lvl_ref.shape[0]
    sub8 = lax.broadcasted_iota(jnp.int32, (sub, 1), 0) % SUBLANES
    ti = lax.broadcasted_iota(jnp.int32, (C, GW), 0)
    si = lax.broadcasted_iota(jnp.int32, (C, GW), 1) % HEAD
    strict = ti > si
    incl = ti >= si

    def head_sums(t):
        return jnp.concatenate([jnp.dot(t[:, i * GW:(i + 1) * GW].astype(BF16), bdm, preferred_element_type=F32)
                                for i in groups], axis=1)

    def bd_build(t, m):
        zero = jnp.zeros((C, 128), BF16)
        blocks = []
        for hd in range(GROUP):
            col = hd * HEAD // 128
            cols = [zero] * (GW // 128)
            cols[col] = t[:, col * 128:(col + 1) * 128] * m[hd * HEAD % 128 // HEAD]
            blocks.append(jnp.concatenate(cols, axis=1))
        return jnp.concatenate(blocks, axis=0)

    halfm = (halfm_ref[0], halfm_ref[1])
    zero128 = jnp.zeros((C, 128), BF16)
    halff = (halfm[0].astype(F32), halfm[1].astype(F32))
    assert C == HEAD and 2 * HEAD == 128

    def bd_of(t):
        return bd_build(t, halfm)

    pro_out = {}
    last_rows = {}
    res = {}
    ys = {}
    state = {}

    def pro(u):
        h = _rms(x_ref[u * sub:(u + 1) * sub, :], g_ref[...]).astype(BF16)
        p = jnp.dot(h, w_ref[...], preferred_element_type=F32) + b_ref[...]
        yield
        first = carry_ref[...] if u == 0 else last_rows[u - 1]
        last_rows[u] = p[sub - SUBLANES:sub, :]
        prev = _shift_rows(p, first, 1, sub8)
        p = p + (prev - p) * mu_ref[...]
        r = p[:, :c1]
        k = p[:, c1:c2]
        v = p[:, c2:c3]
        lora = p[:, c3:c3 + 128]
        pg = p[:, c3 + 128:]
        yield
        w_pre = w0_ref[...] + _bdot(jnp.tanh(lora), wup_ref[...])
        ew = math.exp(-0.5) * _sigmoid(w_pre)
        yield
        a = _sigmoid(a0_ref[...] + _bdot(lora, aup_ref[...]))
        g = _bdot(_sigmoid(pg), gup_ref[...])
        yield
        kk = k * kk_ref[...]
        k2 = k * (1.0 + (a - 1.0) * ka_ref[...])
        kk = kk * jnp.minimum(lax.rsqrt(head_sums(kk * kk)), 1e12)
        yield
        bonus = head_sums(r * k2 * rk_ref[...]) * v
        yield
        cl = _split2_dot_left(tri_ref[...], ew)
        yield
        w_inv = jnp.exp(cl)
        at = -kk * jnp.exp(ew - cl)
        yield
        rt = r * jnp.exp(-cl)
        bt = (kk * a) * w_inv
        kt = k2 * w_inv
        pro_out[u] = (at, rt, bt, kt, v, cl, bonus, g, bt.T, kt.T)

    def indep(u, c, gi):
        at, rt, bt, kt, v, cl = pro_out[u][:6]
        btT, ktT = pro_out[u][8:]
        rs = slice(c * C, (c + 1) * C)
        ls = slice(gi * GW, (gi + 1) * GW)
        a_c, r_c, b_c, k_c, v_c = at[rs, ls], rt[rs, ls], bt[rs, ls], kt[rs, ls], v[rs, ls]
        lhs = jnp.concatenate([a_c, r_c], axis=0).astype(BF16)

        def bd_t(xt):
            blocks = []
            for hd in range(GROUP):
                r0 = gi * GW + hd * HEAD
                col = xt[r0:r0 + HEAD, (c // 2) * 128:(c // 2 + 1) * 128]
                if hd % 2 != c % 2:
                    col = pltpu.roll(col, HEAD, 1)
                cols = [zero128] * (GW // 128)
                cols[hd * HEAD // 128] = col.astype(BF16) * halfm[hd % 2]
                blocks.append(jnp.concatenate(cols, axis=1))
            return jnp.concatenate(blocks, axis=0)

        rhs = jnp.concatenate([bd_t(btT), bd_t(ktT)], axis=1)
        A = jnp.dot(lhs, rhs, preferred_element_type=F32)
        yield
        n = jnp.where(strict, A[:C, :GW], 0.0)
        aak = jnp.where(strict, A[:C, GW:], 0.0).astype(BF16)
        arbk = jnp.concatenate([jnp.where(incl, A[C:, :GW], 0.0), jnp.where(incl, A[C:, GW:], 0.0)],
                               axis=1).astype(BF16)
        n_bf = n.astype(BF16)
        T = eye_pk + n * base_pk
        for lv in range(nlev):
            TM = jnp.dot(T.astype(BF16), bd_build(n_bf, (lvl_ref[lv, 0], lvl_ref[lv, 1])),
                         preferred_element_type=F32)
            yield
            T = T + jnp.dot(TM.astype(BF16), bd_of(T.astype(BF16)), preferred_element_type=F32)
            yield
        v_bf = v_c.astype(BF16)
        bdv = bd_of(v_bf)
        AV = jnp.dot(aak, bdv, preferred_element_type=F32)
        yield
        wc = jnp.exp(-cl[(c + 1) * C - 1:(c + 1) * C, ls])
        hat = jnp.concatenate([b_c * wc, k_c * wc], axis=0).astype(BF16)
        wcol = jnp.sum(eye_gw * wc, axis=1, keepdims=True)
        res[(u, c, gi)] = (lhs, T.astype(BF16), AV, arbk, bdv, hat, v_bf, wcol)

    def expand(pc):
        blocks = []
        for hd in range(GROUP):
            cols = [zero128] * (GW // 128)
            cols[hd * HEAD // 128] = pc[hd * HEAD:(hd + 1) * HEAD, :]
            blocks.append(jnp.concatenate(cols, axis=1))
        return jnp.concatenate(blocks, axis=0)

    def compact(m):
        return jnp.concatenate([m[hd * HEAD:(hd + 1) * HEAD, (hd * HEAD // 128) * 128:(hd * HEAD // 128 + 1) * 128]
                                * halff[hd % 2] for hd in range(GROUP)], axis=0)

    def dep(u, gi):
        P = p_ref[gi] if u == 0 else state[gi]
        for c in range(nchunks):
            lhs, T_bf, AV, arbk, bdv, hat, v_bf, wcol = res[(u, c, gi)]
            AP = jnp.dot(lhs, expand(P.astype(BF16)), preferred_element_type=F32)
            yield
            U = jnp.dot(T_bf, bd_of((AP[:C] + AV).astype(BF16)), preferred_element_type=F32)
            yield
            U_bf = U.astype(BF16)
            ys[(u, c, gi)] = AP[C:] + jnp.dot(arbk, jnp.concatenate([bd_of(U_bf), bdv], axis=0),
                                              preferred_element_type=F32)
            upd = lax.dot_general(hat, jnp.concatenate([U_bf, v_bf], axis=0), (((0,), (0,)), ((), ())),
                                  preferred_element_type=F32)
            yield
            P = P * wcol + compact(upd)
        state[gi] = P
        if u == nsub - 1:
            p_ref[gi] = P

    def epi(u):
        bonus, g = pro_out[u][6:8]
        y = jnp.concatenate([jnp.concatenate([ys[(u, c, gi)] for gi in groups], axis=1)
                             for c in range(nchunks)], axis=0)
        inv_n = 1.0 / HEAD
        mean = head_sums(y) * inv_n
        yield
        yc = y - mean
        var = head_sums(yc * yc) * inv_n
        yield
        yn = yc * lax.rsqrt(var + LNX_EPS) * lg_ref[...] + lb_ref[...]
        o_ref[u * sub:(u + 1) * sub, :] = ((yn + bonus) * g).astype(o_ref.dtype)

    for step in range(nsub + 3):
        gens = []
        if step < nsub:
            gens.append(pro(step))
        if 0 <= step - 1 < nsub:
            gens += [indep(step - 1, c, gi) for c in range(nchunks) for gi in groups]
        if 0 <= step - 2 < nsub:
            gens += [dep(step - 2, gi) for gi in groups]
        if 0 <= step - 3 < nsub:
            gens.append(epi(step - 3))
        _round_robin(gens)
    carry_ref[...] = last_rows[nsub - 1]


def _rwkv_call(x, g, w, b, mu, w0, wup, a0, aup, gup, k_k, k_a, r_k, lnx_g, lnx_b, *, ts, sub=256):
    sub = min(sub, ts)
    B, S, D = x.shape
    width = w0.shape[-1]
    cols = w.shape[-1]
    C, GW = CHUNK, GROUP * HEAD
    bdf = jnp.kron(jnp.eye(GROUP, dtype=F32), jnp.ones((HEAD, HEAD), F32))
    ti = jnp.arange(C)[:, None]
    si = jnp.arange(C)[None, :]
    levels = []
    m = 2
    while m < C:
        lm = ((ti // m) % 2 == 1) & ((si // m) % 2 == 0) & (ti // (2 * m) == si // (2 * m))
        lm = lm.astype(F32)
        levels.append(jnp.stack([jnp.concatenate([lm, 0 * lm], axis=1), jnp.concatenate([0 * lm, lm], axis=1)]))
        m *= 2
    lvl = jnp.stack(levels).astype(BF16)
    ones = jnp.ones((C, HEAD), F32)
    halfm = jnp.stack([jnp.concatenate([ones, 0 * ones], axis=1),
                       jnp.concatenate([0 * ones, ones], axis=1)]).astype(BF16)
    eye_pk = jnp.tile(jnp.eye(C, dtype=F32), (1, GROUP))
    base_pk = jnp.tile(((ti // 2 == si // 2) & (ti > si)).astype(F32), (1, GROUP))
    tri = jnp.kron(jnp.eye(sub // C, dtype=F32), jnp.tril(jnp.ones((C, C), F32))).astype(BF16)
    row = lambda t: t.reshape(1, -1).astype(F32)
    consts = [row(g), w.astype(BF16), row(b), row(mu), row(w0), wup, row(a0), aup, gup,
              row(k_k), row(k_a), row(r_k), row(lnx_g), row(lnx_b),
              bdf.astype(BF16), lvl, halfm, eye_pk, base_pk, tri, jnp.eye(GW, dtype=F32)]
    body = functools.partial(_rwkv_body, ts=ts, sub=sub, width=width)
    return pl.pallas_call(
        body,
        grid=(B, S // ts),
        in_specs=[pl.BlockSpec((None, ts, D), lambda i, j: (i, j, 0))] + [_const_spec(c.shape) for c in consts],
        out_specs=pl.BlockSpec((None, ts, width), lambda i, j: (i, j, 0)),
        out_shape=jax.ShapeDtypeStruct((B, S, width), BF16),
        scratch_shapes=[pltpu.VMEM((SUBLANES, cols), F32), pltpu.VMEM((width // GW, GW, 128), F32)],
        compiler_params=pltpu.CompilerParams(dimension_semantics=("arbitrary", "arbitrary"),
                                             vmem_limit_bytes=VMEM_LIMIT),
        name="rwkv",
    )(x, *consts)


def _mix_body(x_ref, ya_ref, g_ref, wr_ref, br_ref, cw_ref, cb_ref, wgr_ref, bgr_ref, lam_ref,
              wg_ref, bg_ref, wa_ref, wb_ref, wm_ref, o_ref, xs_ref, hc_ref,
              *, ts, sub, width, d, kconv):
    s = pl.program_id(1)
    nsub = ts // sub

    @pl.when(s == 0)
    def _():
        xs_ref[...] = jnp.zeros_like(xs_ref)
        hc_ref[...] = jnp.zeros_like(hc_ref)

    row = lax.broadcasted_iota(jnp.int32, (sub, 1), 0)
    sub8 = row % SUBLANES
    lam = lam_ref[...]
    log_sig = LRU_C * (jnp.minimum(lam, 0.0) - jnp.log(1.0 + jnp.exp(-jnp.abs(lam))))
    hs, ybs, heads_out, carries, tails = {}, {}, {}, {}, {}

    def rg(u):
        x = x_ref[u * sub:(u + 1) * sub, :]
        h = _rms(x, g_ref[...]).astype(BF16)
        hs[u] = h
        yield
        pxy = jnp.dot(h, wr_ref[...], preferred_element_type=F32) + br_ref[...]
        px = pxy[:, :width]
        py = pxy[:, width:]
        yield
        prev8 = xs_ref[...] if u == 0 else tails[u - 1]
        tails[u] = px[sub - SUBLANES:sub, :]
        xb = cb_ref[...] + px * cw_ref[kconv - 1:kconv, :]
        for j in range(1, kconv):
            xb = xb + _shift_rows(px, prev8, j, sub8) * cw_ref[kconv - 1 - j:kconv - j, :]
        yield
        xb_bf = xb.astype(BF16)
        half = width // 2
        g0 = jnp.dot(xb_bf[:, :half], wgr_ref[0], preferred_element_type=F32)
        g1 = jnp.dot(xb_bf[:, half:], wgr_ref[1], preferred_element_type=F32)
        ga = _sigmoid(jnp.concatenate([g0[:, :half], g1[:, :half]], axis=1) + bgr_ref[:, :width])
        gx = _sigmoid(jnp.concatenate([g0[:, half:], g1[:, half:]], axis=1) + bgr_ref[:, width:])
        yield
        a = jnp.exp(ga * log_sig)
        mult = _sqrt(1.0 - a * a)
        if u == 0:
            mult = jnp.where((row == 0) & (s == 0), 1.0, mult)
        uu = xb * gx * mult
        yield
        dd = 1
        while dd < SUBLANES:
            keep = sub8 >= dd
            u_sh = _group_roll(uu, dd)
            a_sh = _group_roll(a, dd)
            uu = uu + jnp.where(keep, a * u_sh, 0.0)
            a = jnp.where(keep, a * a_sh, a)
            dd *= 2
            yield
        h_prev = hc_ref[...] if u == 0 else carries[u - 1]
        gelu_y = _gelu(py)
        yield
        outs = []
        for r0 in range(0, sub, SUBLANES):
            hg = uu[r0:r0 + SUBLANES, :] + a[r0:r0 + SUBLANES, :] * h_prev
            h_prev = hg[SUBLANES - 1:SUBLANES, :]
            outs.append(hg)
        carries[u] = h_prev
        ybs[u] = (jnp.concatenate(outs, axis=0) * gelu_y).astype(BF16)

    def head(u):
        for _ in range(3):
            yield
        h = hs[u]
        half = d // 2
        parts = []
        for c0 in range(0, 2 * d, half):
            parts.append(jnp.dot(h, wg_ref[:, c0:c0 + half], preferred_element_type=F32) + bg_ref[:, c0:c0 + half])
            yield
        ya = jnp.dot(ya_ref[u * sub:(u + 1) * sub, :], wa_ref[...], preferred_element_type=F32)
        heads_out[u] = (jnp.concatenate(parts[:2], axis=1), jnp.concatenate(parts[2:], axis=1), ya)

    def tail(u):
        ga, gb, ya = heads_out[u]
        yb = jnp.dot(ybs[u], wb_ref[...], preferred_element_type=F32)
        yield
        merged = (_sigmoid(ga) * ya + _sigmoid(gb) * yb).astype(BF16)
        yield
        o_ref[u * sub:(u + 1) * sub, :] = (x_ref[u * sub:(u + 1) * sub, :]
                                           + jnp.dot(merged, wm_ref[...], preferred_element_type=F32))

    for step in range(nsub + 1):
        gens = []
        if step < nsub:
            gens += [rg(step), head(step)]
        if step >= 1:
            gens.append(tail(step - 1))
        _round_robin(gens)
    hc_ref[...] = carries[nsub - 1]
    xs_ref[...] = tails[nsub - 1]


def _mix_call(x, ya, g, wr, br, conv_w, conv_b, w_ga, b_ga, w_gx, b_gx, lam, wg, bg, wa, wb, wm, *, ts, sub=256):
    sub = min(sub, ts)
    B, S, D = x.shape
    width = lam.shape[-1]
    nb = w_ga.shape[0]
    bdiag = lambda t, lo, hi: jax.scipy.linalg.block_diag(*[t[i] for i in range(lo, hi)])
    wgr = jnp.stack([jnp.concatenate([bdiag(w_ga, lo, lo + nb // 2), bdiag(w_gx, lo, lo + nb // 2)], axis=1)
                     for lo in (0, nb // 2)]).astype(BF16)
    row = lambda t: t.reshape(1, -1).astype(F32)
    consts = [row(g), wr.astype(BF16), row(br), conv_w.astype(F32), row(conv_b), wgr,
              jnp.concatenate([row(b_ga), row(b_gx)], axis=1), row(lam),
              wg.astype(BF16), row(bg), wa.astype(BF16), wb.astype(BF16), wm.astype(BF16)]
    body = functools.partial(_mix_body, ts=ts, sub=sub, width=width, d=D, kconv=conv_w.shape[0])
    tok = lambda n: pl.BlockSpec((None, ts, n), lambda i, j: (i, j, 0))
    return pl.pallas_call(
        body,
        grid=(B, S // ts),
        in_specs=[tok(D), tok(ya.shape[-1])] + [_const_spec(c.shape) for c in consts],
        out_specs=tok(D),
        out_shape=jax.ShapeDtypeStruct((B, S, D), F32),
        scratch_shapes=[pltpu.VMEM((SUBLANES, width), F32), pltpu.VMEM((1, width), F32)],
        compiler_params=pltpu.CompilerParams(dimension_semantics=("arbitrary", "arbitrary"),
                                             vmem_limit_bytes=VMEM_LIMIT),
        name="mix",
    )(x, ya, *consts)


def _attn_body(x_ref, m_ref, gm_ref, wkv_ref, g_ref, wq_ref, wo_ref, o_ref, kv_ref, *, d, heads):
    @pl.when(pl.program_id(1) == 0)
    def _():
        hm = _rms(m_ref[...], gm_ref[...]).astype(BF16)
        kv_ref[...] = jnp.dot(hm, wkv_ref[...], preferred_element_type=F32).astype(kv_ref.dtype)

    x = x_ref[...]
    h = _rms(x, g_ref[...]).astype(BF16)
    q = jnp.dot(h, wq_ref[...], preferred_element_type=F32)
    hd = d // heads
    scale = hd ** -0.5
    hr = range(heads)
    sc = [lax.dot_general(q[:, i * hd:(i + 1) * hd].astype(BF16), kv_ref[:, i * hd:(i + 1) * hd],
                          (((1,), (1,)), ((), ())), preferred_element_type=F32) * scale for i in hr]
    e = [jnp.exp(sc[i] - jnp.max(sc[i], axis=-1, keepdims=True)) for i in hr]
    pr = [(e[i] / jnp.sum(e[i], axis=-1, keepdims=True)).astype(BF16) for i in hr]
    outs = [jnp.dot(pr[i], kv_ref[:, d + i * hd:d + (i + 1) * hd], preferred_element_type=F32).astype(BF16)
            for i in hr]
    o = jnp.concatenate(outs, axis=1)
    o_ref[...] = x + jnp.dot(o, wo_ref[...], preferred_element_type=F32)


def _attn_call(x, mem, gm, wkv, g, wq, wo, *, tm, heads):
    B, S, D = x.shape
    M = mem.shape[1]
    consts = [gm.reshape(1, -1).astype(F32), wkv.astype(BF16), g.reshape(1, -1).astype(F32),
              wq.astype(BF16), wo.astype(BF16)]
    tok = pl.BlockSpec((None, tm, D), lambda i, j: (i, j, 0))
    return pl.pallas_call(
        functools.partial(_attn_body, d=D, heads=heads),
        grid=(B, S // tm),
        in_specs=[tok, pl.BlockSpec((None, M, D), lambda i, j: (i, 0, 0))]
        + [_const_spec(c.shape) for c in consts],
        out_specs=tok,
        out_shape=jax.ShapeDtypeStruct((B, S, D), F32),
        scratch_shapes=[pltpu.VMEM((M, 2 * D), BF16)],
        compiler_params=pltpu.CompilerParams(dimension_semantics=("arbitrary", "arbitrary"),
                                             vmem_limit_bytes=VMEM_LIMIT),
        name="attn",
    )(x, mem, *consts)


def _ffn_body(x_ref, g_ref, wi_ref, cw_ref, cb_ref, wo_ref, gf_ref, o_ref, us_ref, act_ref,
              *, tm, dff, ck, kconv, final_norm):
    s = pl.program_id(1)

    @pl.when(s == 0)
    def _():
        us_ref[0:SUBLANES, :] = jnp.zeros((SUBLANES, dff), F32)

    x = x_ref[...]
    h = _rms(x, g_ref[...]).astype(BF16)
    def proj_in(c0):
        return (jnp.dot(h, wi_ref[:, c0:c0 + ck], preferred_element_type=F32),
                jnp.dot(h, wi_ref[:, dff + c0:dff + c0 + ck], preferred_element_type=F32))

    starts = list(range(0, dff, ck))
    nxt = proj_in(starts[0])
    for i, c0 in enumerate(starts):
        ug, uv = nxt
        if i + 1 < len(starts):
            nxt = proj_in(starts[i + 1])
        us_ref[SUBLANES:SUBLANES + tm, c0:c0 + ck] = ug
        gate = cb_ref[:, c0:c0 + ck]
        for j in range(kconv):
            gate = gate + (us_ref[pl.ds(SUBLANES - (kconv - 1) + j, tm), c0:c0 + ck]
                           * cw_ref[j:j + 1, c0:c0 + ck])
        us_ref[0:SUBLANES, c0:c0 + ck] = ug[tm - SUBLANES:tm, :]
        act_ref[:, c0:c0 + ck] = (_gelu(gate) * uv).astype(BF16)
    acc = x + jnp.dot(act_ref[...], wo_ref[...], preferred_element_type=F32)
    if final_norm:
        acc = _rms(acc, gf_ref[...])
    o_ref[...] = acc


def _ffn_call(x, g, wi, conv_w, conv_b, wo, gf, *, tm, ck, final_norm):
    B, S, D = x.shape
    dff = wo.shape[0]
    consts = [g.reshape(1, -1).astype(F32), wi.astype(BF16), conv_w.astype(F32),
              conv_b.reshape(1, -1).astype(F32), wo.astype(BF16), gf.reshape(1, -1).astype(F32)]
    tok = pl.BlockSpec((None, tm, D), lambda i, j: (i, j, 0))
    body = functools.partial(_ffn_body, tm=tm, dff=dff, ck=ck, kconv=conv_w.shape[0], final_norm=final_norm)
    return pl.pallas_call(
        body,
        grid=(B, S // tm),
        in_specs=[tok] + [_const_spec(c.shape) for c in consts],
        out_specs=tok,
        out_shape=jax.ShapeDtypeStruct((B, S, D), F32),
        scratch_shapes=[pltpu.VMEM((tm + SUBLANES, dff), F32), pltpu.VMEM((tm, dff), BF16)],
        compiler_params=pltpu.CompilerParams(dimension_semantics=("arbitrary", "arbitrary"),
                                             vmem_limit_bytes=VMEM_LIMIT),
        name="ffn",
    )(x, *consts)


def _pick(n, pref):
    t = min(n, pref)
    while n % t:
        t //= 2
    return t


def kernel(x, mem, norm_mix_g, w_in, b_in, mu_shift, w0, w_lora_up, a0, a_lora_up, g_lora_up, k_k, k_a, r_k, lnx_g, lnx_b, w_branch_a, conv_b_w, conv_b_b, w_rg_a, b_rg_a, w_rg_x, b_rg_x, lru_lambda, w_branch_b, w_mix_out, norm_x_g, norm_mem_g, w_cq, w_ckv, w_co, norm_ffn_g, w_ffn_in, ffn_conv_w, ffn_conv_b, w_ffn_out, norm_final_g):
    depth = w_in.shape[0]
    B, S, D = x.shape
    a_width = w0.shape[-1]
    b_width = lru_lambda.shape[-1]
    n_dec, n_aaa = w_lora_up.shape[1], a_lora_up.shape[1]
    assert n_dec + n_aaa == 128, "decay/a lora widths must fill one 128-lane slab"
    cb0 = mu_shift.shape[-1]
    cb2 = cb0 + 2 * b_width
    heads = 4
    for l in range(depth):
        wup = jnp.concatenate([w_lora_up[l], jnp.zeros((n_aaa, a_width), F32)], axis=0).astype(BF16)
        aup = jnp.concatenate([jnp.zeros((n_dec, a_width), F32), a_lora_up[l]], axis=0).astype(BF16)
        ya = _rwkv_call(x, norm_mix_g[l], w_in[l][:, :cb0], b_in[l][:cb0], mu_shift[l], w0[l], wup, a0[l], aup,
                        g_lora_up[l].astype(BF16), k_k[l], k_a[l], r_k[l], lnx_g[l], lnx_b[l], ts=_pick(S, 512))
        x = _mix_call(x, ya, norm_mix_g[l], w_in[l][:, cb0:cb2], b_in[l][cb0:cb2], conv_b_w[l], conv_b_b[l],
                      w_rg_a[l], b_rg_a[l], w_rg_x[l], b_rg_x[l], lru_lambda[l], w_in[l][:, cb2:], b_in[l][cb2:],
                      w_branch_a[l], w_branch_b[l], w_mix_out[l], ts=_pick(S, 512), sub=512)
        x = _attn_call(x, mem, norm_mem_g[l], w_ckv[l], norm_x_g[l], w_cq[l], w_co[l], tm=_pick(S, 1024),
                       heads=heads)
        last = l == depth - 1
        x = _ffn_call(x, norm_ffn_g[l], w_ffn_in[l], ffn_conv_w[l], ffn_conv_b[l], w_ffn_out[l], norm_final_g,
                      tm=_pick(S, 512), ck=256, final_norm=last)
    return x
```
